```python
import math
import jax, jax.numpy as jnp
from jax import lax
import numpy as np

D_MODEL = 1024
BATCH = 4
SEQ = 8192
DEPTH = 4

CTX_LEN = 256
GRID_W = 64
N_MIXERS = 3
N_DA = (DEPTH + 2) // 3
N_SW = (DEPTH + 1) // 3
N_SSD = DEPTH // 3
N_MOD = 6
NORM_EPS = 1e-6
ROPE_THETA = 10000.0
NEG_INF = -1e30

DA_HEADS = 8
DA_HEAD_DIM = 64
DA_WIDTH = 2 * DA_HEADS * DA_HEAD_DIM
DA_SCALE = DA_HEAD_DIM ** -0.5
DA_Q_BLOCK = 128

SW_HEADS = 16
SW_KV_HEADS = 4
SW_GROUP = SW_HEADS // SW_KV_HEADS
SW_HEAD_DIM = 64
SW_Q_WIDTH = SW_HEADS * SW_HEAD_DIM
SW_KV_WIDTH = SW_KV_HEADS * SW_HEAD_DIM
SW_WINDOW = 128
SW_BLOCK = 128
SW_SCALE = SW_HEAD_DIM ** -0.5

SSD_INNER = 2 * D_MODEL
SSD_HEAD_DIM = 64
SSD_HEADS = SSD_INNER // SSD_HEAD_DIM
SSD_GROUPS = 4
SSD_HPG = SSD_HEADS // SSD_GROUPS
SSD_STATE = 128
SSD_CONV = 5
SSD_CHUNK = 128
SSD_BC = SSD_GROUPS * SSD_STATE
SSD_CONV_CH = SSD_INNER + 2 * SSD_BC
SSD_IN = SSD_INNER + SSD_CONV_CH + 2 * SSD_HEADS

FFN_HIDDEN = -(-8 * D_MODEL // (3 * 256)) * 256

kernel_name = 'hybrid_interleaved_diffusion_trunk'


def rmsnorm(t, g):
    tf = t.astype(jnp.float32)
    tf = tf * lax.rsqrt(jnp.mean(tf * tf, axis=-1, keepdims=True) + NORM_EPS)
    return (tf * g.astype(jnp.float32)).astype(t.dtype)


def modulate(t, shift, scale):
    return t * (1.0 + scale) + shift


def rope_tables(rows, cols, head_dim):
    n_freq = head_dim // 4
    freqs = ROPE_THETA ** (-jnp.arange(n_freq, dtype=jnp.float32) / n_freq)
    ang = jnp.concatenate([rows[:, None].astype(jnp.float32) * freqs,
                           cols[:, None].astype(jnp.float32) * freqs], axis=-1)
    return jnp.cos(ang), jnp.sin(ang)


def apply_rope(t, cos, sin):
    half = t.shape[-1] // 2
    shp = (1, cos.shape[0]) + (1,) * (t.ndim - 3) + (half,)
    cs = cos.reshape(shp).astype(t.dtype)
    sn = sin.reshape(shp).astype(t.dtype)
    t1, t2 = t[..., :half], t[..., half:]
    return jnp.concatenate([t1 * cs - t2 * sn, t2 * cs + t1 * sn], axis=-1)


def diff_attend(q, k, v, lam):
    s = jnp.einsum('bqhmd,bkhmd->bhmqk', q, k).astype(jnp.float32) * DA_SCALE
    p = jax.nn.softmax(s, axis=-1)
    w = (p[:, :, 0] - lam * p[:, :, 1]).astype(v.dtype)
    return jnp.einsum('bhqk,bkhv->bqhv', w, v)


def diff_attn_mixer(u_ctx, u_lat, w_qkv, w_o, lam_p, subln_g, lam_init, cos, sin, want_ctx):
    b, S, _ = u_lat.shape
    Lc = u_ctx.shape[1]
    qkv = u_lat @ w_qkv
    q = apply_rope(qkv[..., :DA_WIDTH].reshape(b, S, DA_HEADS, 2, DA_HEAD_DIM), cos, sin)
    k = apply_rope(qkv[..., DA_WIDTH:2 * DA_WIDTH].reshape(b, S, DA_HEADS, 2, DA_HEAD_DIM), cos, sin)
    v = qkv[..., 2 * DA_WIDTH:].reshape(b, S, DA_HEADS, 2 * DA_HEAD_DIM)
    kv_c = u_ctx @ w_qkv[:, DA_WIDTH:]
    k_c = kv_c[..., :DA_WIDTH].reshape(b, Lc, DA_HEADS, 2, DA_HEAD_DIM)
    v_c = kv_c[..., DA_WIDTH:].reshape(b, Lc, DA_HEADS, 2 * DA_HEAD_DIM)
    lp = lam_p.astype(jnp.float32)
    lam = jnp.exp(jnp.sum(lp[0] * lp[1])) - jnp.exp(jnp.sum(lp[2] * lp[3])) + lam_init
    k_all = jnp.concatenate([k_c, k], axis=1)
    v_all = jnp.concatenate([v_c, v], axis=1)
    q_blocks = jnp.moveaxis(q.reshape(b, S // DA_Q_BLOCK, DA_Q_BLOCK, DA_HEADS, 2, DA_HEAD_DIM), 1, 0)
    o = lax.map(lambda qb: diff_attend(qb, k_all, v_all, lam), q_blocks)
    o = jnp.moveaxis(o, 0, 1).reshape(b, S, DA_HEADS, 2 * DA_HEAD_DIM)

    def finish(o_heads):
        o_n = rmsnorm(o_heads, subln_g) * (1.0 - lam_init)
        return o_n.reshape(o_n.shape[0], o_n.shape[1], DA_WIDTH) @ w_o

    o_lat = finish(o)
    o_ctx = None
    if want_ctx:
        q_c = (u_ctx @ w_qkv[:, :DA_WIDTH]).reshape(b, Lc, DA_HEADS, 2, DA_HEAD_DIM)
        o_ctx = finish(diff_attend(q_c, k_c, v_c, lam))
    return o_ctx, o_lat


def sink_softmax(s, sink):
    col = jnp.broadcast_to(sink, s.shape[:-1] + (1,))
    return jax.nn.softmax(jnp.concatenate([col, s], axis=-1), axis=-1)[..., 1:]


def band_blocks(t):
    b, S, hk, d = t.shape
    tb = jnp.pad(t.reshape(b, S // SW_BLOCK, SW_BLOCK, hk, d), ((0, 0), (1, 1), (0, 0), (0, 0), (0, 0)))
    return jnp.concatenate([tb[:, :-2], tb[:, 1:-1], tb[:, 2:]], axis=2)


def window_gqa_mixer(u_ctx, u_lat, w_qkv, w_o, sink, cos, sin, want_ctx):
    b, S, _ = u_lat.shape
    Lc = u_ctx.shape[1]
    nb = S // SW_BLOCK
    qkv = u_lat @ w_qkv
    q = apply_rope(qkv[..., :SW_Q_WIDTH].reshape(b, S, SW_KV_HEADS, SW_GROUP, SW_HEAD_DIM), cos, sin)
    k = apply_rope(qkv[..., SW_Q_WIDTH:SW_Q_WIDTH + SW_KV_WIDTH].reshape(b, S, SW_KV_HEADS, SW_HEAD_DIM), cos, sin)
    v = qkv[..., SW_Q_WIDTH + SW_KV_WIDTH:].reshape(b, S, SW_KV_HEADS, SW_HEAD_DIM)
    kv_c = u_ctx @ w_qkv[:, SW_Q_WIDTH:]
    k_c = kv_c[..., :SW_KV_WIDTH].reshape(b, Lc, SW_KV_HEADS, SW_HEAD_DIM)
    v_c = kv_c[..., SW_KV_WIDTH:].reshape(b, Lc, SW_KV_HEADS, SW_HEAD_DIM)
    sink_hg = sink.astype(jnp.float32).reshape(SW_KV_HEADS, SW_GROUP)[:, :, None, None]
    blk = jnp.arange(nb)[:, None, None]
    qpos = blk * SW_BLOCK + jnp.arange(SW_BLOCK)[None, :, None]
    kpos = (blk - 1) * SW_BLOCK + jnp.arange(3 * SW_BLOCK)[None, None, :]
    valid = (jnp.abs(qpos - kpos) <= SW_WINDOW) & (kpos >= 0) & (kpos < S)
    qb = q.reshape(b, nb, SW_BLOCK, SW_KV_HEADS, SW_GROUP, SW_HEAD_DIM)
    k_band, v_band = band_blocks(k), band_blocks(v)
    s_loc = jnp.einsum('bnqhgd,bnkhd->bnhgqk', qb, k_band).astype(jnp.float32) * SW_SCALE
    s_loc = jnp.where(valid[None, :, None, None], s_loc, NEG_INF)
    s_ctx = jnp.einsum('bnqhgd,bchd->bnhgqc', qb, k_c).astype(jnp.float32) * SW_SCALE
    p = sink_softmax(jnp.concatenate([s_ctx, s_loc], axis=-1), sink_hg).astype(v.dtype)
    o = (jnp.einsum('bnhgqc,bchd->bnqhgd', p[..., :Lc], v_c)
         + jnp.einsum('bnhgqk,bnkhd->bnqhgd', p[..., Lc:], v_band))
    o_lat = o.reshape(b, S, SW_Q_WIDTH) @ w_o
    o_ctx = None
    if want_ctx:
        q_c = (u_ctx @ w_qkv[:, :SW_Q_WIDTH]).reshape(b, Lc, SW_KV_HEADS, SW_GROUP, SW_HEAD_DIM)
        s_c = jnp.einsum('bqhgd,bkhd->bhgqk', q_c, k_c).astype(jnp.float32) * SW_SCALE
        p_c = sink_softmax(s_c, sink_hg).astype(v.dtype)
        o_ctx = jnp.einsum('bhgqk,bkhd->bqhgd', p_c, v_c).reshape(b, Lc, SW_Q_WIDTH) @ w_o
    return o_ctx, o_lat


def dwconv_centred(u, w, bias):
    ch = u.shape[-1]
    out = lax.conv_general_dilated(u, w[:, None, :].astype(u.dtype), window_strides=(1,),
                                   padding=((SSD_CONV // 2, SSD_CONV // 2),),
                                   dimension_numbers=('NWC', 'WIO', 'NWC'), feature_group_count=ch)
    return out + bias.astype(u.dtype)


def ssd_scan(x, dt, A, B, C, init_state, need_y):
    b, L, H, P = x.shape
    nc = L // SSD_CHUNK
    shp = (b, nc, SSD_CHUNK, SSD_GROUPS, SSD_HPG)
    xdt = (x.astype(jnp.float32) * dt[..., None]).reshape(shp + (P,))
    a_cs = jnp.cumsum((dt * A).reshape(shp), axis=2)
    Bc = B.astype(jnp.float32).reshape(b, nc, SSD_CHUNK, SSD_GROUPS, SSD_STATE)
    Cc = C.astype(jnp.float32).reshape(b, nc, SSD_CHUNK, SSD_GROUPS, SSD_STATE)
    decay_to_end = jnp.exp(a_cs[:, :, -1:] - a_cs)
    chunk_states = jnp.einsum('bclgn,bclgh,bclghp->bcghpn', Bc, decay_to_end, xdt)
    chunk_decay = jnp.exp(a_cs[:, :, -1])

    def step(state, inp):
        dec, st = inp
        return state * dec[..., None, None] + st, state

    final, prev = lax.scan(step, init_state.reshape(b, SSD_GROUPS, SSD_HPG, P, SSD_STATE),
                           (jnp.moveaxis(chunk_decay, 1, 0), jnp.moveaxis(chunk_states, 1, 0)))
    final = final.reshape(b, H, P, SSD_STATE)
    if not need_y:
        return None, final
    a_t = jnp.moveaxis(a_cs, 2, -1)
    seg = a_t[..., :, None] - a_t[..., None, :]
    tril = jnp.tril(jnp.ones((SSD_CHUNK, SSD_CHUNK), dtype=bool))
    decay_in = jnp.exp(jnp.where(tril, seg, -jnp.inf))
    cb = jnp.einsum('bclgn,bcsgn->bcgls', Cc, Bc)
    y_diag = jnp.einsum('bcgls,bcghls,bcsghp->bclghp', cb, decay_in, xdt)
    y_off = jnp.einsum('bclgn,bcghpn,bclgh->bclghp', Cc, jnp.moveaxis(prev, 0, 1), jnp.exp(a_cs))
    return (y_diag + y_off).reshape(b, L, H, P).astype(x.dtype), final


def ssd_mixer(u_ctx, u_lat, w_in, conv_w, conv_b, a_log, dt_bias, d_skip, norm_g, w_out, want_ctx):
    A = -jnp.exp(a_log.astype(jnp.float32))

    def project(u):
        b, L, _ = u.shape
        zxbcdt = u @ w_in
        z = zxbcdt[..., :SSD_INNER]
        xbc = jax.nn.silu(dwconv_centred(zxbcdt[..., SSD_INNER:SSD_INNER + SSD_CONV_CH], conv_w, conv_b))
        dt = jax.nn.softplus(zxbcdt[..., SSD_INNER + SSD_CONV_CH:].astype(jnp.float32).reshape(b, L, 2, SSD_HEADS)
                             + dt_bias.astype(jnp.float32))
        xs = xbc[..., :SSD_INNER].reshape(b, L, SSD_HEADS, SSD_HEAD_DIM)
        Bm = xbc[..., SSD_INNER:SSD_INNER + SSD_BC].reshape(b, L, SSD_GROUPS, SSD_STATE)
        Cm = xbc[..., SSD_INNER + SSD_BC:].reshape(b, L, SSD_GROUPS, SSD_STATE)
        return z, xs, Bm, Cm, dt

    def flip(t):
        return jnp.flip(t, axis=1)

    def finish(y_f, y_b, xs, z):
        b, L = z.shape[:2]
        y = y_f + flip(y_b) + xs * d_skip.astype(xs.dtype)[:, None]
        y = y.reshape(b, L, SSD_INNER) * jax.nn.silu(z)
        y = rmsnorm(y.reshape(b, L, SSD_GROUPS, SSD_INNER // SSD_GROUPS), norm_g.reshape(SSD_GROUPS, -1))
        return y.reshape(b, L, SSD_INNER) @ w_out

    z_c, x_c, B_c, C_c, dt_c = project(u_ctx)
    z_l, x_l, B_l, C_l, dt_l = project(u_lat)
    zero = jnp.zeros((u_ctx.shape[0], SSD_HEADS, SSD_HEAD_DIM, SSD_STATE), jnp.float32)
    yf_c, sf_c = ssd_scan(x_c, dt_c[:, :, 0], A[0], B_c, C_c, zero, want_ctx)
    yb_c, sb_c = ssd_scan(flip(x_c), flip(dt_c[:, :, 1]), A[1], flip(B_c), flip(C_c), zero, want_ctx)
    yf_l, _ = ssd_scan(x_l, dt_l[:, :, 0], A[0], B_l, C_l, sf_c, True)
    yb_l, _ = ssd_scan(flip(x_l), flip(dt_l[:, :, 1]), A[1], flip(B_l), flip(C_l), sb_c, True)
    o_lat = finish(yf_l, yb_l, x_l, z_l)
    o_ctx = finish(yf_c, yb_c, x_c, z_c) if want_ctx else None
    return o_ctx, o_lat


def swiglu(u, w_in, w_out):
    gu = u @ w_in
    return (jax.nn.silu(gu[..., :FFN_HIDDEN]) * gu[..., FFN_HIDDEN:]) @ w_out


def setup_inputs(seed: int = 0) -> dict:
    key = jax.random.key(seed)
    ks = jax.random.split(key, 24)

    def nrm(k, shape, scale):
        return jax.random.normal(k, shape, jnp.float32) * scale

    D = D_MODEL
    dt0 = jnp.exp(jax.random.uniform(ks[20], (N_SSD, 2, SSD_HEADS), jnp.float32,
                                     minval=math.log(1e-3), maxval=math.log(1e-1)))
    return {
        'x': nrm(ks[0], (BATCH, SEQ, D), 1.0),
        'c': nrm(ks[1], (BATCH, D), 1.0),
        'ctx': nrm(ks[2], (BATCH, CTX_LEN, D), 1.0),
        'c_ctx': nrm(ks[3], (D,), 1.0),
        'ada_w': nrm(ks[4], (DEPTH, D, N_MOD * D), 0.5 * D ** -0.5),
        'ada_b': nrm(ks[5], (DEPTH, N_MOD * D), 0.02),
        'norm_g': 1.0 + nrm(ks[6], (DEPTH, 4, D), 0.05),
        'ffn_w_in': nrm(ks[7], (DEPTH, D, 2 * FFN_HIDDEN), D ** -0.5),
        'ffn_w_out': nrm(ks[8], (DEPTH, FFN_HIDDEN, D), FFN_HIDDEN ** -0.5),
        'da_w_qkv': nrm(ks[9], (N_DA, D, 3 * DA_WIDTH), D ** -0.5),
        'da_w_o': nrm(ks[10], (N_DA, DA_WIDTH, D), DA_WIDTH ** -0.5),
        'da_lambda': nrm(ks[11], (N_DA, 4, DA_HEAD_DIM), 0.1),
        'da_subln': 1.0 + nrm(ks[12], (N_DA, 2 * DA_HEAD_DIM), 0.05),
        'sw_w_qkv': nrm(ks[13], (N_SW, D, SW_Q_WIDTH + 2 * SW_KV_WIDTH), D ** -0.5),
        'sw_w_o': nrm(ks[14], (N_SW, SW_Q_WIDTH, D), SW_Q_WIDTH ** -0.5),
        'sw_sink': nrm(ks[15], (N_SW, SW_HEADS), 0.5),
        'ssd_w_in': nrm(ks[16], (N_SSD, D, SSD_IN), D ** -0.5),
        'ssd_conv_w': nrm(ks[17], (N_SSD, SSD_CONV, SSD_CONV_CH), SSD_CONV ** -0.5),
        'ssd_conv_b': nrm(ks[18], (N_SSD, SSD_CONV_CH), 0.02),
        'ssd_a_log': jnp.log(jax.random.uniform(ks[19], (N_SSD, 2, SSD_HEADS), jnp.float32, minval=1.0, maxval=16.0)),
        'ssd_dt_bias': dt0 + jnp.log(-jnp.expm1(-dt0)),
        'ssd_d_skip': 1.0 + nrm(ks[21], (N_SSD, SSD_HEADS), 0.1),
        'ssd_norm': 1.0 + nrm(ks[22], (N_SSD, SSD_INNER), 0.05),
        'ssd_w_out': nrm(ks[23], (N_SSD, SSD_INNER, D), SSD_INNER ** -0.5),
    }


def reference(x, c, ctx, c_ctx, ada_w, ada_b, norm_g, ffn_w_in, ffn_w_out, da_w_qkv, da_w_o, da_lambda,
              da_subln, sw_w_qkv, sw_w_o, sw_sink, ssd_w_in, ssd_conv_w, ssd_conv_b, ssd_a_log, ssd_dt_bias,
              ssd_d_skip, ssd_norm, ssd_w_out):
    b, S, D = x.shape
    n_rows = S // GRID_W
    rows = jnp.repeat(jnp.arange(n_rows), GRID_W)
    cols = jnp.tile(jnp.arange(GRID_W), n_rows)
    cos_da, sin_da = rope_tables(rows, cols, DA_HEAD_DIM)
    cos_sw, sin_sw = rope_tables(rows, cols, SW_HEAD_DIM)
    act_lat = jax.nn.silu(c)
    act_ctx = jax.nn.silu(c_ctx)
    h_lat, h_ctx = x, ctx
    for i in range(DEPTH):
        want_ctx = i < DEPTH - 1
        j = i // N_MIXERS
        kind = i % N_MIXERS
        m_lat = (act_lat @ ada_w[i] + ada_b[i]).reshape(b, N_MOD, 1, D)
        m_ctx = (act_ctx @ ada_w[i] + ada_b[i]).reshape(N_MOD, D)
        u_lat = modulate(rmsnorm(h_lat, norm_g[i, 0]), m_lat[:, 0], m_lat[:, 1])
        u_ctx = modulate(rmsnorm(h_ctx, norm_g[i, 0]), m_ctx[0], m_ctx[1])
        if kind == 0:
            lam_init = 0.8 - 0.6 * math.exp(-0.3 * i)
            o_ctx, o_lat = diff_attn_mixer(u_ctx, u_lat, da_w_qkv[j], da_w_o[j], da_lambda[j], da_subln[j],
                                           lam_init, cos_da, sin_da, want_ctx)
        elif kind == 1:
            o_ctx, o_lat = window_gqa_mixer(u_ctx, u_lat, sw_w_qkv[j], sw_w_o[j], sw_sink[j],
                                            cos_sw, sin_sw, want_ctx)
        else:
            o_ctx, o_lat = ssd_mixer(u_ctx, u_lat, ssd_w_in[j], ssd_conv_w[j], ssd_conv_b[j], ssd_a_log[j],
                                     ssd_dt_bias[j], ssd_d_skip[j], ssd_norm[j], ssd_w_out[j], want_ctx)
        h_lat = h_lat + m_lat[:, 2] * rmsnorm(o_lat, norm_g[i, 1])
        u_lat = modulate(rmsnorm(h_lat, norm_g[i, 2]), m_lat[:, 3], m_lat[:, 4])
        h_lat = h_lat + m_lat[:, 5] * rmsnorm(swiglu(u_lat, ffn_w_in[i], ffn_w_out[i]), norm_g[i, 3])
        if want_ctx:
            h_ctx = h_ctx + m_ctx[2] * rmsnorm(o_ctx, norm_g[i, 1])
            u_ctx_f = modulate(rmsnorm(h_ctx, norm_g[i, 2]), m_ctx[3], m_ctx[4])
            h_ctx = h_ctx + m_ctx[5] * rmsnorm(swiglu(u_ctx_f, ffn_w_in[i], ffn_w_out[i]), norm_g[i, 3])
    return h_lat
```

```python
import functools
import math

import numpy as np
import jax
import jax.numpy as jnp
from jax import lax
from jax.experimental import pallas as pl
from jax.experimental.pallas import tpu as pltpu

F32 = jnp.float32
BF16 = jnp.bfloat16

NORM_EPS = 1e-6
ROPE_THETA = 10000.0
NEG_INF = -1e30
GRID_W = 64
N_MOD = 6
N_MIXERS = 3

LANES = 128
SUBLANES = 8
VMEM_LIMIT = 56 * 1024 * 1024

HEAD_DIM = 64
DA_HEADS = 8
SW_HEADS = 16
SW_KV_HEADS = 4
SW_GROUP = SW_HEADS // SW_KV_HEADS
SW_WINDOW = 128
SW_BLOCK = 128
SSD_HEAD_DIM = 64
SSD_GROUPS = 4
SSD_STATE = 128
SSD_CONV = 5
SSD_CHUNK = 128
CONV_HALO = 8


def _cparams(sem):
    return pltpu.CompilerParams(dimension_semantics=sem, vmem_limit_bytes=VMEM_LIMIT)


def _resident(shape):
    nd = len(shape)
    return pl.BlockSpec(shape, lambda *_: (0,) * nd, pipeline_mode=pl.Buffered(1))


def _rms(t):
    return t * lax.rsqrt(jnp.mean(t * t, axis=-1, keepdims=True) + NORM_EPS)


def _silu(t):
    return t * jax.nn.sigmoid(t)


def _ada_kernel(c_ref, w_ref, b_ref, o_ref):
    act = _silu(c_ref[...])
    o_ref[...] = jnp.dot(act, w_ref[...], precision=lax.Precision.HIGHEST,
                         preferred_element_type=F32) + b_ref[...]


def ada_modulation(cond, ada_w, ada_b):
    L, D, N = ada_w.shape
    R = cond.shape[0]
    tn = 1536
    out = pl.pallas_call(
        _ada_kernel,
        grid=(L, N // tn),
        in_specs=[pl.BlockSpec((R, D), lambda l, j: (0, 0)),
                  pl.BlockSpec((None, D, tn), lambda l, j: (l, 0, j)),
                  pl.BlockSpec((None, 1, tn), lambda l, j: (l, 0, j))],
        out_specs=pl.BlockSpec((None, R, tn), lambda l, j: (l, 0, j)),
        out_shape=jax.ShapeDtypeStruct((L, R, N), F32),
        compiler_params=_cparams(("arbitrary", "arbitrary")),
        name="ada_modulation",
    )(cond, ada_w, ada_b.reshape(L, 1, N))
    return out.reshape(L, R, N_MOD, D)


class Layout:
    def __init__(self, B, S, Lc, tm):
        assert S % tm == 0 and (B * Lc) % tm == 0
        self.B, self.S, self.Lc, self.tm = B, S, Lc, tm
        self.T = B * S + B * Lc
        self.npb = S // tm
        self.nlat = B * self.npb
        self.nblk = self.T // tm

    def mod_idx(self, i):
        return jnp.minimum(i // self.npb, self.B)

    def pos_idx(self, i):
        return jnp.where(i < self.nlat, i % self.npb, self.npb)


def _rope_slab(y, cos, sin_signed, lane):
    fwd = pltpu.roll(y, LANES - HEAD_DIM // 2, 1)
    bwd = pltpu.roll(y, HEAD_DIM // 2, 1)
    rot = jnp.where(lane % HEAD_DIM < HEAD_DIM // 2, fwd, bwd)
    return y * cos + rot * sin_signed


def _in_proj_kernel(*refs, segs, rope_cols, shift_row, scale_row, cw):
    h_ref, mod_ref, g_ref, w_ref = refs[:4]
    k = 4
    if rope_cols:
        cos_ref, sin_ref = refs[4:6]
        k = 6
    out_refs = refs[k:]
    h = h_ref[...]
    u = _rms(h) * g_ref[...]
    u = u * (1.0 + mod_ref[scale_row:scale_row + 1, :]) + mod_ref[shift_row:shift_row + 1, :]
    ub = u.astype(BF16)
    if rope_cols:
        cos = cos_ref[...]
        sin = sin_ref[...]
        lane = lax.broadcasted_iota(jnp.int32, cos.shape, 1)
    for o_ref, (col0, width) in zip(out_refs, segs):
        for c0 in range(0, width, cw):
            w_c = min(cw, width - c0)
            y = jnp.dot(ub, w_ref[:, col0 + c0:col0 + c0 + w_c], preferred_element_type=F32)
            if col0 + c0 < rope_cols:
                assert col0 + c0 + w_c <= rope_cols and w_c % LANES == 0
                y = jnp.concatenate(
                    [_rope_slab(y[:, s:s + LANES], cos, sin, lane) for s in range(0, w_c, LANES)], axis=1)
            o_ref[:, c0:c0 + w_c] = y.astype(o_ref.dtype)


def in_proj(lay, h, mod, g, w, segs, out_dtypes, *, rope=None, rope_cols=0, shift_row=0, scale_row=1,
            cw=512, name="in_proj"):
    tm, D = lay.tm, h.shape[1]
    N = w.shape[1]
    in_specs = [pl.BlockSpec((tm, D), lambda i: (i, 0)),
                pl.BlockSpec((None, N_MOD, D), lambda i: (lay.mod_idx(i), 0, 0)),
                pl.BlockSpec((1, D), lambda i: (0, 0)),
                _resident((D, N))]
    args = [h, mod, g.reshape(1, D), w]
    if rope_cols:
        cos, sin = rope
        in_specs += [pl.BlockSpec((tm, LANES), lambda i: (lay.pos_idx(i), 0))] * 2
        args += [cos, sin]
    out_specs = [pl.BlockSpec((tm, width), lambda i: (i, 0)) for _, width in segs]
    out_shape = [jax.ShapeDtypeStruct((lay.T, width), dt) for (_, width), dt in zip(segs, out_dtypes)]
    kern = functools.partial(_in_proj_kernel, segs=tuple(segs), rope_cols=rope_cols,
                             shift_row=shift_row, scale_row=scale_row, cw=cw)
    return pl.pallas_call(
        kern, grid=(lay.nblk,), in_specs=in_specs, out_specs=out_specs, out_shape=out_shape,
        compiler_params=_cparams(("parallel",)), name=name,
    )(*args)


def _out_proj_kernel(*refs, ssd, gate_row):
    if ssd:
        yf_ref, yb_ref, xbc_ref, z_ref, dskip_ref, gn_ref, w_ref, h_ref, mod_ref, g_ref, o_ref = refs
        inner = yf_ref.shape[1]
        gw = inner // SSD_GROUPS
        y = (yf_ref[...].astype(F32) + yb_ref[...].astype(F32)
             + xbc_ref[:, :inner].astype(F32) * dskip_ref[...])
        y = y * _silu(z_ref[...].astype(F32))
        y = jnp.concatenate([_rms(y[:, k * gw:(k + 1) * gw]) for k in range(SSD_GROUPS)], axis=1)
        a = (y * gn_ref[...]).astype(BF16)
    else:
        a_ref, w_ref, h_ref, mod_ref, g_ref, o_ref = refs
        a = a_ref[...]
    y = jnp.dot(a, w_ref[...], preferred_element_type=F32)
    o_ref[...] = h_ref[...] + mod_ref[gate_row:gate_row + 1, :] * (_rms(y) * g_ref[...])


def out_proj(lay, acts, w, h, mod, g, *, ssd_extra=None, gate_row=2, nblk=None, name="out_proj"):
    tm, D = lay.tm, h.shape[1]
    K = w.shape[0]
    nblk = lay.nblk if nblk is None else nblk
    row = lambda width: pl.BlockSpec((tm, width), lambda i: (i, 0))
    in_specs = [row(a.shape[1]) for a in acts]
    args = list(acts)
    if ssd_extra is not None:
        dskip, gn = ssd_extra
        in_specs += [pl.BlockSpec((1, K), lambda i: (0, 0))] * 2
        args += [dskip, gn]
    in_specs += [_resident((K, D)), row(D),
                 pl.BlockSpec((None, N_MOD, D), lambda i: (lay.mod_idx(i), 0, 0)),
                 pl.BlockSpec((1, D), lambda i: (0, 0))]
    args += [w, h, mod, g.reshape(1, D)]
    kern = functools.partial(_out_proj_kernel, ssd=ssd_extra is not None, gate_row=gate_row)
    return pl.pallas_call(
        kern, grid=(nblk,), in_specs=in_specs, out_specs=row(D),
        out_shape=jax.ShapeDtypeStruct((nblk * tm, D), F32),
        compiler_params=_cparams(("parallel",)), name=name,
    )(*args)


def _ffn_kernel(h_ref, mod_ref, g_ref, w_in_ref, w_out_ref, o_ref, act_ref, *, hidden, cw):
    h = h_ref[...]
    u = _rms(h) * g_ref[0:1, :]
    ub = (u * (1.0 + mod_ref[4:5, :]) + mod_ref[3:4, :]).astype(BF16)
    for c0 in range(0, hidden, cw):
        gate = jnp.dot(ub, w_in_ref[:, c0:c0 + cw], preferred_element_type=F32)
        up = jnp.dot(ub, w_in_ref[:, hidden + c0:hidden + c0 + cw], preferred_element_type=F32)
        act_ref[:, c0:c0 + cw] = (_silu(gate) * up).astype(BF16)
    y = jnp.dot(act_ref[...], w_out_ref[...], preferred_element_type=F32)
    o_ref[...] = h + mod_ref[5:6, :] * (_rms(y) * g_ref[1:2, :])


def ffn(lay, h, mod, g_pair, w_in, w_out, *, nblk=None, name="ffn"):
    tm, D = lay.tm, h.shape[1]
    hidden = w_out.shape[0]
    nblk = lay.nblk if nblk is None else nblk
    cw = 256
    assert hidden % cw == 0
    kern = functools.partial(_ffn_kernel, hidden=hidden, cw=cw)
    return pl.pallas_call(
        kern, grid=(nblk,),
        in_specs=[pl.BlockSpec((tm, D), lambda i: (i, 0)),
                  pl.BlockSpec((None, N_MOD, D), lambda i: (lay.mod_idx(i), 0, 0)),
                  pl.BlockSpec((2, D), lambda i: (0, 0)),
                  _resident((D, 2 * hidden)), _resident((hidden, D))],
        out_specs=pl.BlockSpec((tm, D), lambda i: (i, 0)),
        out_shape=jax.ShapeDtypeStruct((nblk * tm, D), F32),
        scratch_shapes=[pltpu.VMEM((tm, hidden), BF16)],
        compiler_params=_cparams(("parallel",)), name=name,
    )(h, mod, g_pair, w_in, w_out)


def _da_attn_kernel(lam_ref, subln_ref, q_ref, kc_ref, vc_ref, kl_ref, vl_ref, o_ref, *,
                    lam_init, tk, n_lat_q):
    tq = q_ref.shape[0]
    Lc, S = kc_ref.shape[0], kl_ref.shape[0]
    q = q_ref[...]
    lane = lax.broadcasted_iota(jnp.int32, q.shape, 1)
    zero = jnp.zeros_like(q)
    qs = (jnp.where(lane < HEAD_DIM, q, zero), jnp.where(lane >= HEAD_DIM, q, zero))
    nt = (((1,), (1,)), ((), ()))

    def step(k_ref, v_ref, start, size, carry):
        kc = k_ref[pl.ds(start, size), :]
        vc = v_ref[pl.ds(start, size), :]
        new = []
        for mp in range(2):
            m, l, acc = carry[3 * mp:3 * mp + 3]
            s = lax.dot_general(qs[mp], kc, nt, preferred_element_type=F32)
            m_new = jnp.maximum(m, jnp.max(s, axis=-1, keepdims=True))
            alpha = jnp.exp(m - m_new)
            p = jnp.exp(s - m_new)
            l = alpha * l + jnp.sum(p, axis=-1, keepdims=True)
            acc = alpha * acc + jnp.dot(p.astype(BF16), vc, preferred_element_type=F32)
            new += [m_new, l, acc]
        return tuple(new)

    init = []
    for _ in range(2):
        init += [jnp.full((tq, 1), NEG_INF, F32), jnp.zeros((tq, 1), F32), jnp.zeros((tq, LANES), F32)]
    tkc = min(tk, Lc)
    carry = lax.fori_loop(
        0, Lc // tkc,
        lambda c, cr: step(kc_ref, vc_ref, pl.multiple_of(c * tkc, tkc), tkc, cr), tuple(init))
    n_lat = jnp.where(pl.program_id(2) < n_lat_q, S // tk, 0)
    carry = lax.fori_loop(
        0, n_lat,
        lambda c, cr: step(kl_ref, vl_ref, pl.multiple_of(c * tk, tk), tk, cr), carry)

    lp = lam_ref[...]
    lam = (jnp.exp(jnp.sum(lp[0:1] * lp[1:2], axis=-1, keepdims=True))
           - jnp.exp(jnp.sum(lp[2:3] * lp[3:4], axis=-1, keepdims=True)) + lam_init)
    o = carry[2] / carry[1] - lam * (carry[5] / carry[4])
    o = _rms(o) * subln_ref[...] * (1.0 - lam_init)
    o_ref[...] = o.astype(o_ref.dtype)


def da_attention(B, S, Lc, qkv, lam_p, subln, lam_init, want_ctx):
    T = qkv.shape[0]
    width = DA_HEADS * 2 * HEAD_DIM
    tq = min(256, Lc)
    tk = min(512, S)
    nql, nqc = S // tq, Lc // tq
    nq = nql + (nqc if want_ctx else 0)
    H = DA_HEADS

    def q_idx(b, h, i):
        return (jnp.where(i < nql, b * nql + i, B * nql + b * nqc + (i - nql)), h)

    kern = functools.partial(_da_attn_kernel, lam_init=lam_init, tk=tk, n_lat_q=nql)
    return pl.pallas_call(
        kern, grid=(B, H, nq),
        in_specs=[pl.BlockSpec((4, HEAD_DIM), lambda b, h, i: (0, 0)),
                  pl.BlockSpec((1, 2 * HEAD_DIM), lambda b, h, i: (0, 0)),
                  pl.BlockSpec((tq, LANES), q_idx),
                  pl.BlockSpec((Lc, LANES), lambda b, h, i: (B * S // Lc + b, H + h)),
                  pl.BlockSpec((Lc, LANES), lambda b, h, i: (B * S // Lc + b, 2 * H + h)),
                  pl.BlockSpec((S, LANES), lambda b, h, i: (b, H + h)),
                  pl.BlockSpec((S, LANES), lambda b, h, i: (b, 2 * H + h))],
        out_specs=pl.BlockSpec((tq, LANES), q_idx),
        out_shape=jax.ShapeDtypeStruct((T, width), BF16),
        compiler_params=_cparams(("parallel", "parallel", "arbitrary")), name="da_attention",
    )(lam_p, subln.reshape(1, 2 * HEAD_DIM), qkv, qkv, qkv, qkv, qkv)


def _sw_attn_kernel(sink_ref, q_ref, kc_ref, vc_ref, kp_ref, kx_ref, kn_ref, vp_ref, vx_ref, vn_ref,
                    o_ref, *, n_lat_q, nb):
    tq = q_ref.shape[0]
    Lc = kc_ref.shape[0]
    i = pl.program_id(1)
    is_lat = i < n_lat_q
    n = i % nb
    qrow = lax.broadcasted_iota(jnp.int32, (tq, 3 * SW_BLOCK), 0)
    krel = lax.broadcasted_iota(jnp.int32, (tq, 3 * SW_BLOCK), 1) - SW_BLOCK
    kabs = krel + n * SW_BLOCK
    valid = (jnp.abs(qrow - krel) <= SW_WINDOW) & (kabs >= 0) & (kabs < nb * SW_BLOCK) & is_lat
    lane = lax.broadcasted_iota(jnp.int32, (tq, LANES), 1)
    lo = lane < HEAD_DIM
    nt = (((1,), (1,)), ((), ()))
    n_pairs = SW_KV_HEADS // 2
    for j in range(n_pairs):
        ksl = slice(j * LANES, (j + 1) * LANES)
        k_ctx = kc_ref[:, ksl]
        v_ctx = vc_ref[:, ksl]
        k_band = jnp.concatenate([kp_ref[:, ksl], kx_ref[:, ksl], kn_ref[:, ksl]], axis=0)
        v_band = jnp.concatenate([vp_ref[:, ksl], vx_ref[:, ksl], vn_ref[:, ksl]], axis=0)
        for g in range(SW_GROUP):
            slab = j * SW_GROUP + g
            q = q_ref[:, slab * LANES:(slab + 1) * LANES]
            zero = jnp.zeros_like(q)
            halves = []
            for e in range(2):
                sink = sink_ref[(2 * j + e) * SW_GROUP + g]
                qz = jnp.where(lo if e == 0 else ~lo, q, zero)
                s_ctx = lax.dot_general(qz, k_ctx, nt, preferred_element_type=F32)
                s_loc = lax.dot_general(qz, k_band, nt, preferred_element_type=F32)
                s_loc = jnp.where(valid, s_loc, NEG_INF)
                m = jnp.maximum(jnp.maximum(jnp.max(s_ctx, axis=-1, keepdims=True),
                                            jnp.max(s_loc, axis=-1, keepdims=True)), sink)
                p_ctx = jnp.exp(s_ctx - m)
                p_loc = jnp.exp(s_loc - m)
                denom = (jnp.sum(p_ctx, axis=-1, keepdims=True) + jnp.sum(p_loc, axis=-1, keepdims=True)
                         + jnp.exp(sink - m))
                o = (jnp.dot(p_ctx.astype(BF16), v_ctx, preferred_element_type=F32)
                     + jnp.dot(p_loc.astype(BF16), v_band, preferred_element_type=F32))
                halves.append(o / denom)
            o_ref[:, slab * LANES:(slab + 1) * LANES] = jnp.where(lo, halves[0], halves[1]).astype(o_ref.dtype)


def sw_attention(B, S, Lc, qkv, sink, want_ctx):
    T = qkv.shape[0]
    qw = SW_HEADS * HEAD_DIM
    kvw = SW_KV_HEADS * HEAD_DIM
    tq = SW_BLOCK
    nb = S // SW_BLOCK
    nqc = Lc // tq
    nq = nb + (nqc if want_ctx else 0)
    kcol, vcol = qw // kvw, qw // kvw + 1

    def q_idx(b, i):
        return (jnp.where(i < nb, b * nb + i, B * nb + b * nqc + (i - nb)), 0)

    def band(col, off):
        def idx(b, i):
            n = jnp.clip(jnp.where(i < nb, i, 0) + off, 0, nb - 1)
            return (b * nb + n, col)
        return pl.BlockSpec((SW_BLOCK, kvw), idx)

    kern = functools.partial(_sw_attn_kernel, n_lat_q=nb, nb=nb)
    return pl.pallas_call(
        kern, grid=(B, nq),
        in_specs=[pl.BlockSpec(memory_space=pltpu.SMEM),
                  pl.BlockSpec((tq, qw), q_idx),
                  pl.BlockSpec((Lc, kvw), lambda b, i: (B * S // Lc + b, kcol)),
                  pl.BlockSpec((Lc, kvw), lambda b, i: (B * S // Lc + b, vcol)),
                  band(kcol, -1), band(kcol, 0), band(kcol, 1),
                  band(vcol, -1), band(vcol, 0), band(vcol, 1)],
        out_specs=pl.BlockSpec((tq, qw), q_idx),
        out_shape=jax.ShapeDtypeStruct((T, qw), BF16),
        compiler_params=_cparams(("parallel", "arbitrary")), name="sw_attention",
    )(sink, qkv, qkv, qkv, qkv, qkv, qkv, qkv, qkv, qkv)


def _sw_head_perm():
    cols = []
    for j in range(SW_KV_HEADS // 2):
        for g in range(SW_GROUP):
            for e in range(2):
                head = (2 * j + e) * SW_GROUP + g
                cols += list(range(head * HEAD_DIM, (head + 1) * HEAD_DIM))
    return np.asarray(cols, dtype=np.int32)


def _ssd_conv_kernel(prev_ref, cur_ref, next_ref, w_ref, b_ref, o_ref, ext_ref, *, seg_lat, seg_ctx, n_lat_blk):
    tc = cur_ref.shape[0]
    i = pl.program_id(0)
    row0 = i * tc
    is_lat = i < n_lat_blk
    seg = jnp.where(is_lat, seg_lat, seg_ctx)
    rel = jnp.where(is_lat, row0, row0 - n_lat_blk * tc) % seg
    keep_prev = (rel != 0).astype(F32)
    keep_next = (rel + tc != seg).astype(F32)
    ext_ref[0:CONV_HALO, :] = prev_ref[...].astype(F32) * keep_prev
    ext_ref[CONV_HALO:CONV_HALO + tc, :] = cur_ref[...].astype(F32)
    ext_ref[CONV_HALO + tc:, :] = next_ref[...].astype(F32) * keep_next
    acc = jnp.zeros(o_ref.shape, F32) + b_ref[...]
    for k in range(SSD_CONV):
        off = CONV_HALO - SSD_CONV // 2 + k
        acc = acc + ext_ref[off:off + tc, :] * w_ref[k:k + 1, :]
    o_ref[...] = _silu(acc).astype(o_ref.dtype)


def ssd_conv(B, S, Lc, xbc, conv_w, conv_b):
    T, C = xbc.shape
    tc = min(256, Lc)
    cwid = 512
    nblk = T // tc
    hb = tc // CONV_HALO
    nh = T // CONV_HALO
    kern = functools.partial(_ssd_conv_kernel, seg_lat=S, seg_ctx=Lc, n_lat_blk=B * S // tc)
    return pl.pallas_call(
        kern, grid=(nblk, C // cwid),
        in_specs=[pl.BlockSpec((CONV_HALO, cwid), lambda i, j: (jnp.maximum(i * hb - 1, 0), j)),
                  pl.BlockSpec((tc, cwid), lambda i, j: (i, j)),
                  pl.BlockSpec((CONV_HALO, cwid), lambda i, j: (jnp.minimum((i + 1) * hb, nh - 1), j)),
                  pl.BlockSpec((SSD_CONV, cwid), lambda i, j: (0, j)),
                  pl.BlockSpec((1, cwid), lambda i, j: (0, j))],
        out_specs=pl.BlockSpec((tc, cwid), lambda i, j: (i, j)),
        out_shape=jax.ShapeDtypeStruct((T, C), BF16),
        scratch_shapes=[pltpu.VMEM((tc + 2 * CONV_HALO, cwid), F32)],
        compiler_params=_cparams(("parallel", "parallel")), name="ssd_conv",
    )(xbc, xbc, xbc, conv_w, conv_b.reshape(1, C))


def _split3(t):
    hi = t.astype(BF16)
    r1 = t - hi.astype(F32)
    mid = r1.astype(BF16)
    lo = (r1 - mid.astype(F32)).astype(BF16)
    return hi, mid, lo


def _dot01(t, ones_mat):
    out = None
    for part in _split3(t):
        term = jnp.dot(part, ones_mat, preferred_element_type=F32)
        out = term if out is None else out + term
    return out


def _dot_left01(ones_mat, t):
    out = None
    for part in _split3(t):
        term = jnp.dot(ones_mat, part, preferred_element_type=F32)
        out = term if out is None else out + term
    return out


def _ssd_scan_kernel(xf_ref, xb_ref, dtf_ref, dtb_ref, aneg_ref, bias_ref, ef_ref, eb_ref,
                     yf_ref, yb_ref, sf_ref, sb_ref, *, inner, n_heads):
    @pl.when(pl.program_id(1) == 0)
    def _():
        sf_ref[...] = jnp.zeros_like(sf_ref)
        sb_ref[...] = jnp.zeros_like(sb_ref)

    for reverse, x_ref, dt_ref, e_ref, y_ref, s_ref in (
            (False, xf_ref, dtf_ref, ef_ref, yf_ref, sf_ref),
            (True, xb_ref, dtb_ref, eb_ref, yb_ref, sb_ref)):
        _ssd_chunk(x_ref, dt_ref, aneg_ref, bias_ref, e_ref, y_ref, s_ref,
                   reverse=reverse, inner=inner, n_heads=n_heads)


def _ssd_chunk(x_ref, dt_ref, aneg_ref, bias_ref, e_ref, y_ref, s_ref, *, reverse, inner, n_heads):
    L = x_ref.shape[0]
    gn = SSD_GROUPS * SSD_STATE
    hpg = n_heads // SSD_GROUPS
    gw = hpg * SSD_HEAD_DIM
    head0 = n_heads if reverse else 0

    z = dt_ref[...] + bias_ref[...]
    dt = jnp.maximum(z, 0.0) + jnp.log1p(jnp.exp(-jnp.abs(z)))
    a = dt * (-jnp.exp(aneg_ref[...]))
    r = lax.broadcasted_iota(jnp.int32, (L, L), 0)
    c = lax.broadcasted_iota(jnp.int32, (L, L), 1)
    causal = (c >= r) if reverse else (c <= r)
    tri = jnp.where(causal, 1.0, 0.0).astype(BF16)
    acs = _dot_left01(tri, a)
    a_tot = jnp.sum(a, axis=0, keepdims=True)
    dte = jnp.exp(a_tot - acs)
    eacs = jnp.exp(acs)
    acs_t = acs.T

    e = e_ref[...]
    dt_e = _dot01(dt, e)
    dtdte_e = _dot01(dt * dte, e)
    eacs_e = _dot01(eacs, e)
    cdec_e = _dot01(jnp.broadcast_to(jnp.exp(a_tot), (SUBLANES, LANES)), e)[0:1, :]

    x = x_ref[:, :inner].astype(F32)
    xdt = (x * dt_e).astype(BF16)
    xw = (x * dtdte_e).astype(BF16)
    lane = lax.broadcasted_iota(jnp.int32, (L, LANES), 1)
    lo = lane < SSD_HEAD_DIM
    nt = (((1,), (1,)), ((), ()))
    tn = (((0,), (0,)), ((), ()))
    for g in range(SSD_GROUPS):
        Bg = x_ref[:, inner + g * SSD_STATE:inner + (g + 1) * SSD_STATE]
        Cg = x_ref[:, inner + gn + g * SSD_STATE:inner + gn + (g + 1) * SSD_STATE]
        gs = slice(g * gw, (g + 1) * gw)
        cb = lax.dot_general(Cg, Bg, nt, preferred_element_type=F32)
        s_prev = s_ref[:, gs]
        y_off = jnp.dot(Cg, s_prev.astype(BF16), preferred_element_type=F32) * eacs_e[:, gs]
        for k in range(hpg // 2):
            sl = slice(g * gw + k * LANES, g * gw + (k + 1) * LANES)
            halves = []
            for eh in range(2):
                col = head0 + g * hpg + 2 * k + eh
                seg = acs[:, col:col + 1] - acs_t[col:col + 1, :]
                w = cb * jnp.exp(jnp.where(causal, seg, NEG_INF))
                halves.append(jnp.dot(w.astype(BF16), xdt[:, sl], preferred_element_type=F32))
            y = jnp.where(lo, halves[0], halves[1]) + y_off[:, k * LANES:(k + 1) * LANES]
            y_ref[:, sl] = y.astype(y_ref.dtype)
        s_ref[:, gs] = s_prev * cdec_e[:, gs] + lax.dot_general(Bg, xw[:, gs], tn, preferred_element_type=F32)


def ssd_scan(B, S, Lc, xbc, dt_raw, a_log_pad, dt_bias_pad, inner, n_heads):
    T, C = xbc.shape
    L = SSD_CHUNK
    nl, nc = S // L, Lc // L
    nsteps = nl + nc

    def fwd_idx(b, i):
        return (jnp.where(i < nc, B * nl + b * nc + i, b * nl + (i - nc)), 0)

    def bwd_idx(b, i):
        return (jnp.where(i < nc, B * nl + b * nc + (nc - 1 - i), b * nl + (nl - 1 - (i - nc))), 0)

    hd = np.arange(inner) // SSD_HEAD_DIM
    ef = (np.arange(LANES)[:, None] == hd[None, :]).astype(np.float32)
    eb = (np.arange(LANES)[:, None] == (hd[None, :] + n_heads)).astype(np.float32)
    kern = functools.partial(_ssd_scan_kernel, inner=inner, n_heads=n_heads)
    const = lambda shape: pl.BlockSpec(shape, lambda b, i: (0, 0))
    return pl.pallas_call(
        kern, grid=(B, nsteps),
        in_specs=[pl.BlockSpec((L, C), fwd_idx), pl.BlockSpec((L, C), bwd_idx),
                  pl.BlockSpec((L, LANES), fwd_idx), pl.BlockSpec((L, LANES), bwd_idx),
                  const((1, LANES)), const((1, LANES)), const((LANES, inner)), const((LANES, inner))],
        out_specs=[pl.BlockSpec((L, inner), fwd_idx), pl.BlockSpec((L, inner), bwd_idx)],
        out_shape=[jax.ShapeDtypeStruct((T, inner), BF16)] * 2,
        scratch_shapes=[pltpu.VMEM((SSD_STATE, inner), F32)] * 2,
        compiler_params=_cparams(("parallel", "arbitrary")), name="ssd_scan",
    )(xbc, xbc, dt_raw, dt_raw, a_log_pad, dt_bias_pad, jnp.asarray(ef, BF16), jnp.asarray(eb, BF16))


def _rope_tables(S, tm):
    n_rows = S // GRID_W
    rows = jnp.repeat(jnp.arange(n_rows), GRID_W)
    cols = jnp.tile(jnp.arange(GRID_W), n_rows)
    n_freq = HEAD_DIM // 4
    freqs = ROPE_THETA ** (-jnp.arange(n_freq, dtype=F32) / n_freq)
    ang = jnp.concatenate([rows[:, None].astype(F32) * freqs, cols[:, None].astype(F32) * freqs], axis=-1)
    cos, sin = jnp.cos(ang), jnp.sin(ang)
    cos_t = jnp.concatenate([cos, cos, cos, cos], axis=-1)
    sin_t = jnp.concatenate([-sin, sin, -sin, sin], axis=-1)
    cos_t = jnp.concatenate([cos_t, jnp.ones((tm, LANES), F32)], axis=0)
    sin_t = jnp.concatenate([sin_t, jnp.zeros((tm, LANES), F32)], axis=0)
    return cos_t, sin_t


def kernel(x, c, ctx, c_ctx, ada_w, ada_b, norm_g, ffn_w_in, ffn_w_out, da_w_qkv, da_w_o, da_lambda, da_subln,
           sw_w_qkv, sw_w_o, sw_sink, ssd_w_in, ssd_conv_w, ssd_conv_b, ssd_a_log, ssd_dt_bias, ssd_d_skip,
           ssd_norm, ssd_w_out):
    B, S, D = x.shape
    Lc = ctx.shape[1]
    depth = ada_w.shape[0]
    tm = min(512, math.gcd(S, B * Lc))
    lay = Layout(B, S, Lc, tm)

    n_cond = -(-(B + 1) // SUBLANES) * SUBLANES
    cond = jnp.concatenate([c, c_ctx[None, :], jnp.zeros((n_cond - B - 1, D), F32)], axis=0)
    mods = ada_modulation(cond, ada_w, ada_b)
    rope = _rope_tables(S, tm)

    h = jnp.concatenate([x.reshape(B * S, D), ctx.reshape(B * Lc, D)], axis=0)
    da_width = DA_HEADS * 2 * HEAD_DIM
    sw_qw = SW_HEADS * HEAD_DIM
    sw_kvw = SW_KV_HEADS * HEAD_DIM
    perm = _sw_head_perm()

    for i in range(depth):
        want_ctx = i < depth - 1
        j = i // N_MIXERS
        kind = i % N_MIXERS
        mod = mods[i]
        g = norm_g[i]
        nblk = None if want_ctx else lay.nlat
        if kind == 0:
            lam_init = 0.8 - 0.6 * math.exp(-0.3 * i)
            w = da_w_qkv[j]
            w = jnp.concatenate([w[:, :da_width] * HEAD_DIM ** -0.5, w[:, da_width:]], axis=1).astype(BF16)
            (qkv,) = in_proj(lay, h, mod, g[0], w, [(0, 3 * da_width)], [BF16], rope=rope,
                             rope_cols=2 * da_width, name="da_in_proj")
            o = da_attention(B, S, Lc, qkv, da_lambda[j], da_subln[j], lam_init, want_ctx)
            h = out_proj(lay, [o], da_w_o[j].astype(BF16), h, mod, g[1], nblk=nblk, name="da_out_proj")
        elif kind == 1:
            w = sw_w_qkv[j]
            w = jnp.concatenate([w[:, :sw_qw][:, perm] * HEAD_DIM ** -0.5, w[:, sw_qw:]], axis=1).astype(BF16)
            (qkv,) = in_proj(lay, h, mod, g[0], w, [(0, sw_qw + 2 * sw_kvw)], [BF16], rope=rope,
                             rope_cols=sw_qw + sw_kvw, cw=256, name="sw_in_proj")
            o = sw_attention(B, S, Lc, qkv, sw_sink[j], want_ctx)
            h = out_proj(lay, [o], sw_w_o[j][perm, :].astype(BF16), h, mod, g[1], nblk=nblk, name="sw_out_proj")
        else:
            n_heads = ssd_a_log.shape[2]
            inner = n_heads * SSD_HEAD_DIM
            conv_ch = inner + 2 * SSD_GROUPS * SSD_STATE
            w = ssd_w_in[j]
            w = jnp.pad(w, ((0, 0), (0, LANES - 2 * n_heads))).astype(BF16)
            z, xbc, dt_raw = in_proj(lay, h, mod, g[0], w,
                                     [(0, inner), (inner, conv_ch), (inner + conv_ch, LANES)],
                                     [BF16, BF16, F32], name="ssd_in_proj")
            xbc = ssd_conv(B, S, Lc, xbc, ssd_conv_w[j], ssd_conv_b[j])
            pad = lambda t: jnp.pad(t.reshape(1, 2 * n_heads), ((0, 0), (0, LANES - 2 * n_heads)))
            yf, yb = ssd_scan(B, S, Lc, xbc, dt_raw, pad(ssd_a_log[j]), pad(ssd_dt_bias[j]), inner, n_heads)
            dskip = jnp.repeat(ssd_d_skip[j], SSD_HEAD_DIM).reshape(1, inner)
            h = out_proj(lay, [yf, yb, xbc, z], ssd_w_out[j].astype(BF16), h, mod, g[1],
                         ssd_extra=(dskip, ssd_norm[j].reshape(1, inner)), nblk=nblk, name="ssd_out_proj")
        h = ffn(lay, h, mod, g[2:4], ffn_w_in[i].astype(BF16), ffn_w_out[i].astype(BF16), nblk=nblk)
    return h[:B * S].reshape(B, S, D)
```

```python
import functools
import math

import numpy as np
import jax
import jax.numpy as jnp
from jax import lax
from jax.experimental import pallas as pl
from jax.experimental.pallas import tpu as pltpu

F32 = jnp.float32
BF16 = jnp.bfloat16

NORM_EPS = 1e-6
ROPE_THETA = 10000.0
NEG_INF = -1e30
GRID_W = 64
N_MOD = 6
N_MIXERS = 3

LANES = 128
SUBLANES = 8
VMEM_LIMIT = 56 * 1024 * 1024

HEAD_DIM = 64
DA_HEADS = 8
SW_HEADS = 16
SW_KV_HEADS = 4
SW_GROUP = SW_HEADS // SW_KV_HEADS
SW_WINDOW = 128
SW_BLOCK = 128
SSD_HEAD_DIM = 64
SSD_GROUPS = 4
SSD_STATE = 128
SSD_CONV = 5
SSD_CHUNK = 128
CONV_HALO = 8


def _cparams(sem):
    return pltpu.CompilerParams(dimension_semantics=sem, vmem_limit_bytes=VMEM_LIMIT)


def _resident(shape):
    nd = len(shape)
    return pl.BlockSpec(shape, lambda *_: (0,) * nd, pipeline_mode=pl.Buffered(1))


def _rms(t):
    return t * lax.rsqrt(jnp.mean(t * t, axis=-1, keepdims=True) + NORM_EPS)


def _silu(t):
    return t * jax.nn.sigmoid(t)


def _ada_kernel(c_ref, w_ref, b_ref, o_ref):
    act = _silu(c_ref[...])
    o_ref[...] = jnp.dot(act, w_ref[...], precision=lax.Precision.HIGHEST,
                         preferred_element_type=F32) + b_ref[...]


def ada_modulation(cond, ada_w, ada_b):
    L, D, N = ada_w.shape
    R = cond.shape[0]
    tn = 1536
    out = pl.pallas_call(
        _ada_kernel,
        grid=(L, N // tn),
        in_specs=[pl.BlockSpec((R, D), lambda l, j: (0, 0)),
                  pl.BlockSpec((None, D, tn), lambda l, j: (l, 0, j)),
                  pl.BlockSpec((None, 1, tn), lambda l, j: (l, 0, j))],
        out_specs=pl.BlockSpec((None, R, tn), lambda l, j: (l, 0, j)),
        out_shape=jax.ShapeDtypeStruct((L, R, N), F32),
        compiler_params=_cparams(("arbitrary", "arbitrary")),
        name="ada_modulation",
    )(cond, ada_w, ada_b.reshape(L, 1, N))
    return out.reshape(L, R, N_MOD, D)


class Layout:
    def __init__(self, B, S, Lc, tm):
        assert S % tm == 0 and (B * Lc) % tm == 0
        self.B, self.S, self.Lc, self.tm = B, S, Lc, tm
        self.T = B * S + B * Lc
        self.npb = S // tm
        self.nlat = B * self.npb
        self.nblk = self.T // tm

    def mod_idx(self, i):
        return jnp.minimum(i // self.npb, self.B)

    def pos_idx(self, i):
        return jnp.where(i < self.nlat, i % self.npb, self.npb)


def _rope_slab(y, cos, sin_signed, lane):
    fwd = pltpu.roll(y, LANES - HEAD_DIM // 2, 1)
    bwd = pltpu.roll(y, HEAD_DIM // 2, 1)
    rot = jnp.where(lane % HEAD_DIM < HEAD_DIM // 2, fwd, bwd)
    return y * cos + rot * sin_signed


def _in_proj_kernel(*refs, segs, rope_cols, shift_row, scale_row, cw):
    h_ref, mod_ref, g_ref, w_ref = refs[:4]
    k = 4
    if rope_cols:
        cos_ref, sin_ref = refs[4:6]
        k = 6
    out_refs = refs[k:]
    h = h_ref[...]
    u = _rms(h) * g_ref[...]
    u = u * (1.0 + mod_ref[scale_row:scale_row + 1, :]) + mod_ref[shift_row:shift_row + 1, :]
    ub = u.astype(BF16)
    if rope_cols:
        cos = cos_ref[...]
        sin = sin_ref[...]
        lane = lax.broadcasted_iota(jnp.int32, cos.shape, 1)
    for o_ref, (col0, width) in zip(out_refs, segs):
        for c0 in range(0, width, cw):
            w_c = min(cw, width - c0)
            y = jnp.dot(ub, w_ref[:, col0 + c0:col0 + c0 + w_c], preferred_element_type=F32)
            if col0 + c0 < rope_cols:
                assert col0 + c0 + w_c <= rope_cols and w_c % LANES == 0
                y = jnp.concatenate(
                    [_rope_slab(y[:, s:s + LANES], cos, sin, lane) for s in range(0, w_c, LANES)], axis=1)
            o_ref[:, c0:c0 + w_c] = y.astype(o_ref.dtype)


def in_proj(lay, h, mod, g, w, segs, out_dtypes, *, rope=None, rope_cols=0, shift_row=0, scale_row=1,
            cw=512, name="in_proj"):
    tm, D = lay.tm, h.shape[1]
    N = w.shape[1]
    in_specs = [pl.BlockSpec((tm, D), lambda i: (i, 0)),
                pl.BlockSpec((None, N_MOD, D), lambda i: (lay.mod_idx(i), 0, 0)),
                pl.BlockSpec((1, D), lambda i: (0, 0)),
                _resident((D, N))]
    args = [h, mod, g.reshape(1, D), w]
    if rope_cols:
        cos, sin = rope
        in_specs += [pl.BlockSpec((tm, LANES), lambda i: (lay.pos_idx(i), 0))] * 2
        args += [cos, sin]
    out_specs = [pl.BlockSpec((tm, width), lambda i: (i, 0)) for _, width in segs]
    out_shape = [jax.ShapeDtypeStruct((lay.T, width), dt) for (_, width), dt in zip(segs, out_dtypes)]
    kern = functools.partial(_in_proj_kernel, segs=tuple(segs), rope_cols=rope_cols,
                             shift_row=shift_row, scale_row=scale_row, cw=cw)
    return pl.pallas_call(
        kern, grid=(lay.nblk,), in_specs=in_specs, out_specs=out_specs, out_shape=out_shape,
        compiler_params=_cparams(("parallel",)), name=name,
    )(*args)


def _out_proj_kernel(*refs, ssd, gate_row):
    if ssd:
        yf_ref, yb_ref, xbc_ref, z_ref, dskip_ref, gn_ref, w_ref, h_ref, mod_ref, g_ref, o_ref = refs
        inner = yf_ref.shape[1]
        gw = inner // SSD_GROUPS
        y = (yf_ref[...].astype(F32) + yb_ref[...].astype(F32)
             + xbc_ref[:, :inner].astype(F32) * dskip_ref[...])
        y = y * _silu(z_ref[...].astype(F32))
        y = jnp.concatenate([_rms(y[:, k * gw:(k + 1) * gw]) for k in range(SSD_GROUPS)], axis=1)
        a = (y * gn_ref[...]).astype(BF16)
    else:
        a_ref, w_ref, h_ref, mod_ref, g_ref, o_ref = refs
        a = a_ref[...]
    y = jnp.dot(a, w_ref[...], preferred_element_type=F32)
    o_ref[...] = h_ref[...] + mod_ref[gate_row:gate_row + 1, :] * (_rms(y) * g_ref[...])


def out_proj(lay, acts, w, h, mod, g, *, ssd_extra=None, gate_row=2, nblk=None, name="out_proj"):
    tm, D = lay.tm, h.shape[1]
    K = w.shape[0]
    nblk = lay.nblk if nblk is None else nblk
    row = lambda width: pl.BlockSpec((tm, width), lambda i: (i, 0))
    in_specs = [row(a.shape[1]) for a in acts]
    args = list(acts)
    if ssd_extra is not None:
        dskip, gn = ssd_extra
        in_specs += [pl.BlockSpec((1, K), lambda i: (0, 0))] * 2
        args += [dskip, gn]
    in_specs += [_resident((K, D)), row(D),
                 pl.BlockSpec((None, N_MOD, D), lambda i: (lay.mod_idx(i), 0, 0)),
                 pl.BlockSpec((1, D), lambda i: (0, 0))]
    args += [w, h, mod, g.reshape(1, D)]
    kern = functools.partial(_out_proj_kernel, ssd=ssd_extra is not None, gate_row=gate_row)
    return pl.pallas_call(
        kern, grid=(nblk,), in_specs=in_specs, out_specs=row(D),
        out_shape=jax.ShapeDtypeStruct((nblk * tm, D), F32),
        compiler_params=_cparams(("parallel",)), name=name,
    )(*args)


def _ffn_kernel(h_ref, mod_ref, g_ref, w_in_ref, w_out_ref, o_ref, act_ref, *, hidden, cw):
    h = h_ref[...]
    u = _rms(h) * g_ref[0:1, :]
    ub = (u * (1.0 + mod_ref[4:5, :]) + mod_ref[3:4, :]).astype(BF16)
    for c0 in range(0, hidden, cw):
        gate = jnp.dot(ub, w_in_ref[:, c0:c0 + cw], preferred_element_type=F32)
        up = jnp.dot(ub, w_in_ref[:, hidden + c0:hidden + c0 + cw], preferred_element_type=F32)
        act_ref[:, c0:c0 + cw] = (_silu(gate) * up).astype(BF16)
    y = jnp.dot(act_ref[...], w_out_ref[...], preferred_element_type=F32)
    o_ref[...] = h + mod_ref[5:6, :] * (_rms(y) * g_ref[1:2, :])


def ffn(lay, h, mod, g_pair, w_in, w_out, *, nblk=None, name="ffn"):
    tm, D = lay.tm, h.shape[1]
    hidden = w_out.shape[0]
    nblk = lay.nblk if nblk is None else nblk
    cw = 256
    assert hidden % cw == 0
    kern = functools.partial(_ffn_kernel, hidden=hidden, cw=cw)
    return pl.pallas_call(
        kern, grid=(nblk,),
        in_specs=[pl.BlockSpec((tm, D), lambda i: (i, 0)),
                  pl.BlockSpec((None, N_MOD, D), lambda i: (lay.mod_idx(i), 0, 0)),
                  pl.BlockSpec((2, D), lambda i: (0, 0)),
                  _resident((D, 2 * hidden)), _resident((hidden, D))],
        out_specs=pl.BlockSpec((tm, D), lambda i: (i, 0)),
        out_shape=jax.ShapeDtypeStruct((nblk * tm, D), F32),
        scratch_shapes=[pltpu.VMEM((tm, hidden), BF16)],
        compiler_params=_cparams(("parallel",)), name=name,
    )(h, mod, g_pair, w_in, w_out)


def _da_attn_kernel(lam_ref, subln_ref, q_ref, k_ref, vt_ref, o_ref, acc0, acc1, s00, s01, s10, s11,
                    p00, p01, p10, p11, *, lam_init, tk, n_ctx_chunks, n_lat_q):
    tq = q_ref.shape[0]
    accs = (acc0, acc1)
    s_bufs = ((s00, s01), (s10, s11))
    p_bufs = ((p00, p01), (p10, p11))
    qt = q_ref[...].astype(F32).T
    row = lax.broadcasted_iota(jnp.int32, qt.shape, 0)
    qts = (jnp.where(row < HEAD_DIM, qt, 0.0).astype(BF16), jnp.where(row >= HEAD_DIM, qt, 0.0).astype(BF16))

    def scores(c, slot):
        kc = k_ref[pl.ds(pl.multiple_of(c * tk, tk), tk), :]
        for mp in range(2):
            s_bufs[slot][mp][...] = jnp.dot(kc, qts[mp], preferred_element_type=F32)

    def accumulate(c, slot, alphas):
        vt = vt_ref[:, pl.ds(pl.multiple_of(c * tk, tk), tk)]
        for mp in range(2):
            accs[mp][...] = alphas[mp] * accs[mp][...] + jnp.dot(vt, p_bufs[slot][mp][...],
                                                                 preferred_element_type=F32)

    def softmax(slot, stats):
        new, alphas = [], []
        for mp in range(2):
            m, l = stats[2 * mp:2 * mp + 2]
            s = s_bufs[slot][mp][...]
            m_new = jnp.maximum(m, jnp.max(s, axis=0, keepdims=True))
            alpha = jnp.exp2(m - m_new)
            p = jnp.exp2(s - m_new)
            p_bufs[slot][mp][...] = p.astype(BF16)
            new += [m_new, alpha * l + jnp.sum(p, axis=0, keepdims=True)]
            alphas.append(alpha)
        return tuple(new), tuple(alphas)

    for mp in range(2):
        accs[mp][...] = jnp.zeros_like(accs[mp])
        p_bufs[1][mp][...] = jnp.zeros_like(p_bufs[1][mp])
    scores(0, 0)
    stats = (jnp.full((1, tq), NEG_INF, F32), jnp.zeros((1, tq), F32)) * 2
    ones = (jnp.ones((1, tq), F32),) * 2
    n_chunks = jnp.where(pl.program_id(2) < n_lat_q, k_ref.shape[0] // tk, n_ctx_chunks)

    def pair(t, carry):
        stats, alphas1 = carry
        c = 2 * t
        scores(c + 1, 1)
        accumulate(jnp.maximum(c - 1, 0), 1, alphas1)
        stats, alphas0 = softmax(0, stats)
        scores(c + 2, 0)
        accumulate(c, 0, alphas0)
        stats, alphas1 = softmax(1, stats)
        return stats, alphas1

    stats, alphas1 = lax.fori_loop(0, (n_chunks - 1) // 2, pair, (stats, ones))
    last = n_chunks - 1
    accumulate(jnp.maximum(last - 1, 0), 1, alphas1)
    stats, alphas0 = softmax(0, stats)
    accumulate(last, 0, alphas0)

    lp = lam_ref[...]
    lam = (jnp.exp(jnp.sum(lp[0:1] * lp[1:2], axis=-1, keepdims=True))
           - jnp.exp(jnp.sum(lp[2:3] * lp[3:4], axis=-1, keepdims=True)) + lam_init)
    ot = acc0[...] / stats[1] - lam * (acc1[...] / stats[3])
    o = _rms(ot.T) * subln_ref[...] * (1.0 - lam_init)
    o_ref[...] = o.astype(o_ref.dtype)


def da_attention(B, S, Lc, q_src, k_all, vt_all, lam_p, subln, lam_init, want_ctx):
    T = q_src.shape[0]
    width = DA_HEADS * 2 * HEAD_DIM
    tq = min(256, Lc)
    tk = min(256, Lc)
    n_keys = Lc + S
    assert (n_keys // tk) % 2 == 1 and (Lc // tk) % 2 == 1 and n_keys % tk == 0
    nql, nqc = S // tq, Lc // tq
    nq = nql + (nqc if want_ctx else 0)

    def q_idx(b, h, i):
        return (jnp.where(i < nql, b * nql + i, B * nql + b * nqc + (i - nql)), h)

    kern = functools.partial(_da_attn_kernel, lam_init=lam_init, tk=tk, n_ctx_chunks=Lc // tk, n_lat_q=nql)
    return pl.pallas_call(
        kern, grid=(B, DA_HEADS, nq),
        in_specs=[pl.BlockSpec((4, HEAD_DIM), lambda b, h, i: (0, 0)),
                  pl.BlockSpec((1, 2 * HEAD_DIM), lambda b, h, i: (0, 0)),
                  pl.BlockSpec((tq, LANES), q_idx),
                  pl.BlockSpec((n_keys, LANES), lambda b, h, i: (b, h)),
                  pl.BlockSpec((LANES, n_keys), lambda b, h, i: (h, b))],
        out_specs=pl.BlockSpec((tq, LANES), q_idx),
        out_shape=jax.ShapeDtypeStruct((T, width), BF16),
        scratch_shapes=([pltpu.VMEM((LANES, tq), F32)] * 2 + [pltpu.VMEM((tk, tq), F32)] * 4
                        + [pltpu.VMEM((tk, tq), BF16)] * 4),
        compiler_params=_cparams(("parallel", "parallel", "arbitrary")), name="da_attention",
    )(lam_p, subln.reshape(1, 2 * HEAD_DIM), q_src, k_all, vt_all)


def _sw_attn_kernel(sink_ref, q_ref, kc_ref, vc_ref, kp_ref, kx_ref, kn_ref, vp_ref, vx_ref, vn_ref,
                    o_ref, *, n_lat_q, nb):
    tq = q_ref.shape[0]
    Lc = kc_ref.shape[0]
    i = pl.program_id(1)
    is_lat = i < n_lat_q
    n = i % nb
    qrow = lax.broadcasted_iota(jnp.int32, (tq, 3 * SW_BLOCK), 0)
    krel = lax.broadcasted_iota(jnp.int32, (tq, 3 * SW_BLOCK), 1) - SW_BLOCK
    kabs = krel + n * SW_BLOCK
    valid = (jnp.abs(qrow - krel) <= SW_WINDOW) & (kabs >= 0) & (kabs < nb * SW_BLOCK) & is_lat
    lane = lax.broadcasted_iota(jnp.int32, (tq, LANES), 1)
    lo = lane < HEAD_DIM
    nt = (((1,), (1,)), ((), ()))
    n_pairs = SW_KV_HEADS // 2
    for j in range(n_pairs):
        ksl = slice(j * LANES, (j + 1) * LANES)
        k_ctx = kc_ref[:, ksl]
        v_ctx = vc_ref[:, ksl]
        k_band = jnp.concatenate([kp_ref[:, ksl], kx_ref[:, ksl], kn_ref[:, ksl]], axis=0)
        v_band = jnp.concatenate([vp_ref[:, ksl], vx_ref[:, ksl], vn_ref[:, ksl]], axis=0)
        for g in range(SW_GROUP):
            slab = j * SW_GROUP + g
            q = q_ref[:, slab * LANES:(slab + 1) * LANES]
            zero = jnp.zeros_like(q)
            halves = []
            for e in range(2):
                sink = sink_ref[(2 * j + e) * SW_GROUP + g]
                qz = jnp.where(lo if e == 0 else ~lo, q, zero)
                s_ctx = lax.dot_general(qz, k_ctx, nt, preferred_element_type=F32)
                s_loc = lax.dot_general(qz, k_band, nt, preferred_element_type=F32)
                s_loc = jnp.where(valid, s_loc, NEG_INF)
                m = jnp.maximum(jnp.maximum(jnp.max(s_ctx, axis=-1, keepdims=True),
                                            jnp.max(s_loc, axis=-1, keepdims=True)), sink)
                p_ctx = jnp.exp(s_ctx - m)
                p_loc = jnp.exp(s_loc - m)
                denom = (jnp.sum(p_ctx, axis=-1, keepdims=True) + jnp.sum(p_loc, axis=-1, keepdims=True)
                         + jnp.exp(sink - m))
                o = (jnp.dot(p_ctx.astype(BF16), v_ctx, preferred_element_type=F32)
                     + jnp.dot(p_loc.astype(BF16), v_band, preferred_element_type=F32))
                halves.append(o / denom)
            o_ref[:, slab * LANES:(slab + 1) * LANES] = jnp.where(lo, halves[0], halves[1]).astype(o_ref.dtype)


def sw_attention(B, S, Lc, qkv, sink, want_ctx):
    T = qkv.shape[0]
    qw = SW_HEADS * HEAD_DIM
    kvw = SW_KV_HEADS * HEAD_DIM
    tq = SW_BLOCK
    nb = S // SW_BLOCK
    nqc = Lc // tq
    nq = nb + (nqc if want_ctx else 0)
    kcol, vcol = qw // kvw, qw // kvw + 1

    def q_idx(b, i):
        return (jnp.where(i < nb, b * nb + i, B * nb + b * nqc + (i - nb)), 0)

    def band(col, off):
        def idx(b, i):
            n = jnp.clip(jnp.where(i < nb, i, 0) + off, 0, nb - 1)
            return (b * nb + n, col)
        return pl.BlockSpec((SW_BLOCK, kvw), idx)

    kern = functools.partial(_sw_attn_kernel, n_lat_q=nb, nb=nb)
    return pl.pallas_call(
        kern, grid=(B, nq),
        in_specs=[pl.BlockSpec(memory_space=pltpu.SMEM),
                  pl.BlockSpec((tq, qw), q_idx),
                  pl.BlockSpec((Lc, kvw), lambda b, i: (B * S // Lc + b, kcol)),
                  pl.BlockSpec((Lc, kvw), lambda b, i: (B * S // Lc + b, vcol)),
                  band(kcol, -1), band(kcol, 0), band(kcol, 1),
                  band(vcol, -1), band(vcol, 0), band(vcol, 1)],
        out_specs=pl.BlockSpec((tq, qw), q_idx),
        out_shape=jax.ShapeDtypeStruct((T, qw), BF16),
        compiler_params=_cparams(("parallel", "arbitrary")), name="sw_attention",
    )(sink, qkv, qkv, qkv, qkv, qkv, qkv, qkv, qkv, qkv)


def _sw_head_perm():
    cols = []
    for j in range(SW_KV_HEADS // 2):
        for g in range(SW_GROUP):
            for e in range(2):
                head = (2 * j + e) * SW_GROUP + g
                cols += list(range(head * HEAD_DIM, (head + 1) * HEAD_DIM))
    return np.asarray(cols, dtype=np.int32)


def _ssd_conv_kernel(prev_ref, cur_ref, next_ref, w_ref, b_ref, o_ref, ext_ref, *, seg_lat, seg_ctx, n_lat_blk):
    tc = cur_ref.shape[0]
    i = pl.program_id(0)
    row0 = i * tc
    is_lat = i < n_lat_blk
    seg = jnp.where(is_lat, seg_lat, seg_ctx)
    rel = jnp.where(is_lat, row0, row0 - n_lat_blk * tc) % seg
    keep_prev = (rel != 0).astype(F32)
    keep_next = (rel + tc != seg).astype(F32)
    ext_ref[0:CONV_HALO, :] = prev_ref[...].astype(F32) * keep_prev
    ext_ref[CONV_HALO:CONV_HALO + tc, :] = cur_ref[...].astype(F32)
    ext_ref[CONV_HALO + tc:, :] = next_ref[...].astype(F32) * keep_next
    acc = jnp.zeros(o_ref.shape, F32) + b_ref[...]
    for k in range(SSD_CONV):
        off = CONV_HALO - SSD_CONV // 2 + k
        acc = acc + ext_ref[off:off + tc, :] * w_ref[k:k + 1, :]
    o_ref[...] = _silu(acc).astype(o_ref.dtype)


def ssd_conv(B, S, Lc, xbc, conv_w, conv_b):
    T, C = xbc.shape
    tc = min(256, Lc)
    cwid = 512
    nblk = T // tc
    hb = tc // CONV_HALO
    nh = T // CONV_HALO
    kern = functools.partial(_ssd_conv_kernel, seg_lat=S, seg_ctx=Lc, n_lat_blk=B * S // tc)
    return pl.pallas_call(
        kern, grid=(nblk, C // cwid),
        in_specs=[pl.BlockSpec((CONV_HALO, cwid), lambda i, j: (jnp.maximum(i * hb - 1, 0), j)),
                  pl.BlockSpec((tc, cwid), lambda i, j: (i, j)),
                  pl.BlockSpec((CONV_HALO, cwid), lambda i, j: (jnp.minimum((i + 1) * hb, nh - 1), j)),
                  pl.BlockSpec((SSD_CONV, cwid), lambda i, j: (0, j)),
                  pl.BlockSpec((1, cwid), lambda i, j: (0, j))],
        out_specs=pl.BlockSpec((tc, cwid), lambda i, j: (i, j)),
        out_shape=jax.ShapeDtypeStruct((T, C), BF16),
        scratch_shapes=[pltpu.VMEM((tc + 2 * CONV_HALO, cwid), F32)],
        compiler_params=_cparams(("parallel", "parallel")), name="ssd_conv",
    )(xbc, xbc, xbc, conv_w, conv_b.reshape(1, C))


def _split3(t):
    hi = t.astype(BF16)
    r1 = t - hi.astype(F32)
    mid = r1.astype(BF16)
    lo = (r1 - mid.astype(F32)).astype(BF16)
    return hi, mid, lo


def _dot01(t, ones_mat):
    out = None
    for part in _split3(t):
        term = jnp.dot(part, ones_mat, preferred_element_type=F32)
        out = term if out is None else out + term
    return out


def _dot_left01(ones_mat, t):
    out = None
    for part in _split3(t):
        term = jnp.dot(ones_mat, part, preferred_element_type=F32)
        out = term if out is None else out + term
    return out


def _ssd_scan_kernel(xf_ref, xb_ref, dtf_ref, dtb_ref, aneg_ref, bias_ref, ef_ref, eb_ref,
                     yf_ref, yb_ref, sf_ref, sb_ref, *, inner, n_heads):
    @pl.when(pl.program_id(1) == 0)
    def _():
        sf_ref[...] = jnp.zeros_like(sf_ref)
        sb_ref[...] = jnp.zeros_like(sb_ref)

    for reverse, x_ref, dt_ref, e_ref, y_ref, s_ref in (
            (False, xf_ref, dtf_ref, ef_ref, yf_ref, sf_ref),
            (True, xb_ref, dtb_ref, eb_ref, yb_ref, sb_ref)):
        _ssd_chunk(x_ref, dt_ref, aneg_ref, bias_ref, e_ref, y_ref, s_ref,
                   reverse=reverse, inner=inner, n_heads=n_heads)


def _ssd_chunk(x_ref, dt_ref, aneg_ref, bias_ref, e_ref, y_ref, s_ref, *, reverse, inner, n_heads):
    L = x_ref.shape[0]
    gn = SSD_GROUPS * SSD_STATE
    hpg = n_heads // SSD_GROUPS
    gw = hpg * SSD_HEAD_DIM
    head0 = n_heads if reverse else 0

    z = dt_ref[...] + bias_ref[...]
    dt = jnp.maximum(z, 0.0) + jnp.log1p(jnp.exp(-jnp.abs(z)))
    a = dt * (-jnp.exp(aneg_ref[...]))
    r = lax.broadcasted_iota(jnp.int32, (L, L), 0)
    c = lax.broadcasted_iota(jnp.int32, (L, L), 1)
    causal = (c >= r) if reverse else (c <= r)
    tri = jnp.where(causal, 1.0, 0.0).astype(BF16)
    acs = _dot_left01(tri, a)
    a_tot = jnp.sum(a, axis=0, keepdims=True)
    dte = jnp.exp(a_tot - acs)
    eacs = jnp.exp(acs)
    acs_t = acs.T

    e = e_ref[...]
    dt_e = _dot01(dt, e)
    dtdte_e = _dot01(dt * dte, e)
    eacs_e = _dot01(eacs, e)
    cdec_e = _dot01(jnp.broadcast_to(jnp.exp(a_tot), (SUBLANES, LANES)), e)[0:1, :]

    x = x_ref[:, :inner].astype(F32)
    xdt = (x * dt_e).astype(BF16)
    xw = (x * dtdte_e).astype(BF16)
    lane = lax.broadcasted_iota(jnp.int32, (L, LANES), 1)
    lo = lane < SSD_HEAD_DIM
    nt = (((1,), (1,)), ((), ()))
    tn = (((0,), (0,)), ((), ()))
    for g in range(SSD_GROUPS):
        Bg = x_ref[:, inner + g * SSD_STATE:inner + (g + 1) * SSD_STATE]
        Cg = x_ref[:, inner + gn + g * SSD_STATE:inner + gn + (g + 1) * SSD_STATE]
        gs = slice(g * gw, (g + 1) * gw)
        cb = lax.dot_general(Cg, Bg, nt, preferred_element_type=F32)
        s_prev = s_ref[:, gs]
        y_off = jnp.dot(Cg, s_prev.astype(BF16), preferred_element_type=F32) * eacs_e[:, gs]
        for k in range(hpg // 2):
            sl = slice(g * gw + k * LANES, g * gw + (k + 1) * LANES)
            halves = []
            for eh in range(2):
                col = head0 + g * hpg + 2 * k + eh
                seg = acs[:, col:col + 1] - acs_t[col:col + 1, :]
                w = cb * jnp.exp(jnp.where(causal, seg, NEG_INF))
                halves.append(jnp.dot(w.astype(BF16), xdt[:, sl], preferred_element_type=F32))
            y = jnp.where(lo, halves[0], halves[1]) + y_off[:, k * LANES:(k + 1) * LANES]
            y_ref[:, sl] = y.astype(y_ref.dtype)
        s_ref[:, gs] = s_prev * cdec_e[:, gs] + lax.dot_general(Bg, xw[:, gs], tn, preferred_element_type=F32)


def ssd_scan(B, S, Lc, xbc, dt_raw, a_log_pad, dt_bias_pad, inner, n_heads):
    T, C = xbc.shape
    L = SSD_CHUNK
    nl, nc = S // L, Lc // L
    nsteps = nl + nc

    def fwd_idx(b, i):
        return (jnp.where(i < nc, B * nl + b * nc + i, b * nl + (i - nc)), 0)

    def bwd_idx(b, i):
        return (jnp.where(i < nc, B * nl + b * nc + (nc - 1 - i), b * nl + (nl - 1 - (i - nc))), 0)

    hd = np.arange(inner) // SSD_HEAD_DIM
    ef = (np.arange(LANES)[:, None] == hd[None, :]).astype(np.float32)
    eb = (np.arange(LANES)[:, None] == (hd[None, :] + n_heads)).astype(np.float32)
    kern = functools.partial(_ssd_scan_kernel, inner=inner, n_heads=n_heads)
    const = lambda shape: pl.BlockSpec(shape, lambda b, i: (0, 0))
    return pl.pallas_call(
        kern, grid=(B, nsteps),
        in_specs=[pl.BlockSpec((L, C), fwd_idx), pl.BlockSpec((L, C), bwd_idx),
                  pl.BlockSpec((L, LANES), fwd_idx), pl.BlockSpec((L, LANES), bwd_idx),
                  const((1, LANES)), const((1, LANES)), const((LANES, inner)), const((LANES, inner))],
        out_specs=[pl.BlockSpec((L, inner), fwd_idx), pl.BlockSpec((L, inner), bwd_idx)],
        out_shape=[jax.ShapeDtypeStruct((T, inner), BF16)] * 2,
        scratch_shapes=[pltpu.VMEM((SSD_STATE, inner), F32)] * 2,
        compiler_params=_cparams(("parallel", "arbitrary")), name="ssd_scan",
    )(xbc, xbc, dt_raw, dt_raw, a_log_pad, dt_bias_pad, jnp.asarray(ef, BF16), jnp.asarray(eb, BF16))


def _rope_tables(S, tm):
    n_rows = S // GRID_W
    rows = jnp.repeat(jnp.arange(n_rows), GRID_W)
    cols = jnp.tile(jnp.arange(GRID_W), n_rows)
    n_freq = HEAD_DIM // 4
    freqs = ROPE_THETA ** (-jnp.arange(n_freq, dtype=F32) / n_freq)
    ang = jnp.concatenate([rows[:, None].astype(F32) * freqs, cols[:, None].astype(F32) * freqs], axis=-1)
    cos, sin = jnp.cos(ang), jnp.sin(ang)
    cos_t = jnp.concatenate([cos, cos, cos, cos], axis=-1)
    sin_t = jnp.concatenate([-sin, sin, -sin, sin], axis=-1)
    cos_t = jnp.concatenate([cos_t, jnp.ones((tm, LANES), F32)], axis=0)
    sin_t = jnp.concatenate([sin_t, jnp.zeros((tm, LANES), F32)], axis=0)
    return cos_t, sin_t


def kernel(x, c, ctx, c_ctx, ada_w, ada_b, norm_g, ffn_w_in, ffn_w_out, da_w_qkv, da_w_o, da_lambda, da_subln,
           sw_w_qkv, sw_w_o, sw_sink, ssd_w_in, ssd_conv_w, ssd_conv_b, ssd_a_log, ssd_dt_bias, ssd_d_skip,
           ssd_norm, ssd_w_out):
    B, S, D = x.shape
    Lc = ctx.shape[1]
    depth = ada_w.shape[0]
    tm = min(512, math.gcd(S, B * Lc))
    lay = Layout(B, S, Lc, tm)

    n_cond = -(-(B + 1) // SUBLANES) * SUBLANES
    cond = jnp.concatenate([c, c_ctx[None, :], jnp.zeros((n_cond - B - 1, D), F32)], axis=0)
    mods = ada_modulation(cond, ada_w, ada_b)
    rope = _rope_tables(S, tm)

    h = jnp.concatenate([x.reshape(B * S, D), ctx.reshape(B * Lc, D)], axis=0)
    da_width = DA_HEADS * 2 * HEAD_DIM
    sw_qw = SW_HEADS * HEAD_DIM
    sw_kvw = SW_KV_HEADS * HEAD_DIM
    perm = _sw_head_perm()

    for i in range(depth):
        want_ctx = i < depth - 1
        j = i // N_MIXERS
        kind = i % N_MIXERS
        mod = mods[i]
        g = norm_g[i]
        nblk = None if want_ctx else lay.nlat
        if kind == 0:
            lam_init = 0.8 - 0.6 * math.exp(-0.3 * i)
            w = da_w_qkv[j]
            q_scale = HEAD_DIM ** -0.5 * math.log2(math.e)
            w = jnp.concatenate([w[:, :da_width] * q_scale, w[:, da_width:]], axis=1).astype(BF16)
            (qkv,) = in_proj(lay, h, mod, g[0], w, [(0, 3 * da_width)], [BF16], rope=rope,
                             rope_cols=2 * da_width, name="da_in_proj")
            kv = jnp.concatenate([qkv[B * S:, da_width:].reshape(B, Lc, 2 * da_width),
                                  qkv[:B * S, da_width:].reshape(B, S, 2 * da_width)], axis=1)
            k_all = kv[:, :, :da_width].reshape(B * (Lc + S), da_width)
            vt_all = kv[:, :, da_width:].reshape(B * (Lc + S), da_width).T
            o = da_attention(B, S, Lc, qkv, k_all, vt_all, da_lambda[j], da_subln[j], lam_init, want_ctx)
            h = out_proj(lay, [o], da_w_o[j].astype(BF16), h, mod, g[1], nblk=nblk, name="da_out_proj")
        elif kind == 1:
            w = sw_w_qkv[j]
            w = jnp.concatenate([w[:, :sw_qw][:, perm] * HEAD_DIM ** -0.5, w[:, sw_qw:]], axis=1).astype(BF16)
            (qkv,) = in_proj(lay, h, mod, g[0], w, [(0, sw_qw + 2 * sw_kvw)], [BF16], rope=rope,
                             rope_cols=sw_qw + sw_kvw, cw=256, name="sw_in_proj")
            o = sw_attention(B, S, Lc, qkv, sw_sink[j], want_ctx)
            h = out_proj(lay, [o], sw_w_o[j][perm, :].astype(BF16), h, mod, g[1], nblk=nblk, name="sw_out_proj")
        else:
            n_heads = ssd_a_log.shape[2]
            inner = n_heads * SSD_HEAD_DIM
            conv_ch = inner + 2 * SSD_GROUPS * SSD_STATE
            w = ssd_w_in[j]
            w = jnp.pad(w, ((0, 0), (0, LANES - 2 * n_heads))).astype(BF16)
            z, xbc, dt_raw = in_proj(lay, h, mod, g[0], w,
                                     [(0, inner), (inner, conv_ch), (inner + conv_ch, LANES)],
                                     [BF16, BF16, F32], name="ssd_in_proj")
            xbc = ssd_conv(B, S, Lc, xbc, ssd_conv_w[j], ssd_conv_b[j])
            pad = lambda t: jnp.pad(t.reshape(1, 2 * n_heads), ((0, 0), (0, LANES - 2 * n_heads)))
            yf, yb = ssd_scan(B, S, Lc, xbc, dt_raw, pad(ssd_a_log[j]), pad(ssd_dt_bias[j]), inner, n_heads)
            dskip = jnp.repeat(ssd_d_skip[j], SSD_HEAD_DIM).reshape(1, inner)
            h = out_proj(lay, [yf, yb, xbc, z], ssd_w_out[j].astype(BF16), h, mod, g[1],
                         ssd_extra=(dskip, ssd_norm[j].reshape(1, inner)), nblk=nblk, name="ssd_out_proj")
        h = ffn(lay, h, mod, g[2:4], ffn_w_in[i].astype(BF16), ffn_w_out[i].astype(BF16), nblk=nblk)
    return h[:B * S].reshape(B, S, D)
```

```python
import functools
import math

import numpy as np
import jax
import jax.numpy as jnp
from jax import lax
from jax.experimental import pallas as pl
from jax.experimental.pallas import tpu as pltpu

F32 = jnp.float32
BF16 = jnp.bfloat16

NORM_EPS = 1e-6
ROPE_THETA = 10000.0
NEG_INF = -1e30
GRID_W = 64
N_MOD = 6
N_MIXERS = 3

LANES = 128
SUBLANES = 8
VMEM_LIMIT = 56 * 1024 * 1024

HEAD_DIM = 64
DA_HEADS = 8
SW_HEADS = 16
SW_KV_HEADS = 4
SW_GROUP = SW_HEADS // SW_KV_HEADS
SW_WINDOW = 128
SW_BLOCK = 128
SSD_HEAD_DIM = 64
SSD_GROUPS = 4
SSD_STATE = 128
SSD_CONV = 5
SSD_CHUNK = 128
DA_TQ = 256
DA_TK = (256,)
DA_UNROLL = 8
CONV_HALO = 8


def _cparams(sem):
    return pltpu.CompilerParams(dimension_semantics=sem, vmem_limit_bytes=VMEM_LIMIT)


def _resident(shape):
    nd = len(shape)
    return pl.BlockSpec(shape, lambda *_: (0,) * nd, pipeline_mode=pl.Buffered(1))


def _rms(t):
    return t * lax.rsqrt(jnp.mean(t * t, axis=-1, keepdims=True) + NORM_EPS)


def _silu(t):
    return t * jax.nn.sigmoid(t)


def _ada_kernel(c_ref, w_ref, b_ref, o_ref):
    act = _silu(c_ref[...])
    o_ref[...] = jnp.dot(act, w_ref[...], precision=lax.Precision.HIGHEST,
                         preferred_element_type=F32) + b_ref[...]


def ada_modulation(cond, ada_w, ada_b):
    L, D, N = ada_w.shape
    R = cond.shape[0]
    tn = 1536
    out = pl.pallas_call(
        _ada_kernel,
        grid=(L, N // tn),
        in_specs=[pl.BlockSpec((R, D), lambda l, j: (0, 0)),
                  pl.BlockSpec((None, D, tn), lambda l, j: (l, 0, j)),
                  pl.BlockSpec((None, 1, tn), lambda l, j: (l, 0, j))],
        out_specs=pl.BlockSpec((None, R, tn), lambda l, j: (l, 0, j)),
        out_shape=jax.ShapeDtypeStruct((L, R, N), F32),
        compiler_params=_cparams(("arbitrary", "arbitrary")),
        name="ada_modulation",
    )(cond, ada_w, ada_b.reshape(L, 1, N))
    return out.reshape(L, R, N_MOD, D)


class Layout:
    def __init__(self, B, S, Lc, tm):
        assert S % tm == 0 and (B * Lc) % tm == 0
        self.B, self.S, self.Lc, self.tm = B, S, Lc, tm
        self.T = B * S + B * Lc
        self.npb = S // tm
        self.nlat = B * self.npb
        self.nblk = self.T // tm

    def mod_idx(self, i):
        return jnp.minimum(i // self.npb, self.B)

    def pos_idx(self, i):
        return jnp.where(i < self.nlat, i % self.npb, self.npb)


def _rope_slab(y, cos, sin_signed, lane):
    fwd = pltpu.roll(y, LANES - HEAD_DIM // 2, 1)
    bwd = pltpu.roll(y, HEAD_DIM // 2, 1)
    rot = jnp.where(lane % HEAD_DIM < HEAD_DIM // 2, fwd, bwd)
    return y * cos + rot * sin_signed


def _in_proj_kernel(*refs, segs, rope_cols, shift_row, scale_row, cw):
    h_ref, mod_ref, g_ref, w_ref = refs[:4]
    k = 4
    if rope_cols:
        cos_ref, sin_ref = refs[4:6]
        k = 6
    out_refs = refs[k:]
    h = h_ref[...]
    u = _rms(h) * g_ref[...]
    u = u * (1.0 + mod_ref[scale_row:scale_row + 1, :]) + mod_ref[shift_row:shift_row + 1, :]
    ub = u.astype(BF16)
    if rope_cols:
        cos = cos_ref[...]
        sin = sin_ref[...]
        lane = lax.broadcasted_iota(jnp.int32, cos.shape, 1)
    for o_ref, (col0, width) in zip(out_refs, segs):
        for c0 in range(0, width, cw):
            w_c = min(cw, width - c0)
            y = jnp.dot(ub, w_ref[:, col0 + c0:col0 + c0 + w_c], preferred_element_type=F32)
            if col0 + c0 < rope_cols:
                assert col0 + c0 + w_c <= rope_cols and w_c % LANES == 0
                y = jnp.concatenate(
                    [_rope_slab(y[:, s:s + LANES], cos, sin, lane) for s in range(0, w_c, LANES)], axis=1)
            o_ref[:, c0:c0 + w_c] = y.astype(o_ref.dtype)


def in_proj(lay, h, mod, g, w, segs, out_dtypes, *, rope=None, rope_cols=0, shift_row=0, scale_row=1,
            cw=512, name="in_proj"):
    tm, D = lay.tm, h.shape[1]
    N = w.shape[1]
    in_specs = [pl.BlockSpec((tm, D), lambda i: (i, 0)),
                pl.BlockSpec((None, N_MOD, D), lambda i: (lay.mod_idx(i), 0, 0)),
                pl.BlockSpec((1, D), lambda i: (0, 0)),
                _resident((D, N))]
    args = [h, mod, g.reshape(1, D), w]
    if rope_cols:
        cos, sin = rope
        in_specs += [pl.BlockSpec((tm, LANES), lambda i: (lay.pos_idx(i), 0))] * 2
        args += [cos, sin]
    out_specs = [pl.BlockSpec((tm, width), lambda i: (i, 0)) for _, width in segs]
    out_shape = [jax.ShapeDtypeStruct((lay.T, width), dt) for (_, width), dt in zip(segs, out_dtypes)]
    kern = functools.partial(_in_proj_kernel, segs=tuple(segs), rope_cols=rope_cols,
                             shift_row=shift_row, scale_row=scale_row, cw=cw)
    return pl.pallas_call(
        kern, grid=(lay.nblk,), in_specs=in_specs, out_specs=out_specs, out_shape=out_shape,
        compiler_params=_cparams(("parallel",)), name=name,
    )(*args)


def _out_proj_kernel(*refs, ssd, gate_row):
    if ssd:
        yf_ref, yb_ref, xbc_ref, z_ref, dskip_ref, gn_ref, w_ref, h_ref, mod_ref, g_ref, o_ref = refs
        inner = yf_ref.shape[1]
        gw = inner // SSD_GROUPS
        y = (yf_ref[...].astype(F32) + yb_ref[...].astype(F32)
             + xbc_ref[:, :inner].astype(F32) * dskip_ref[...])
        y = y * _silu(z_ref[...].astype(F32))
        y = jnp.concatenate([_rms(y[:, k * gw:(k + 1) * gw]) for k in range(SSD_GROUPS)], axis=1)
        a = (y * gn_ref[...]).astype(BF16)
    else:
        a_ref, w_ref, h_ref, mod_ref, g_ref, o_ref = refs
        a = a_ref[...]
    y = jnp.dot(a, w_ref[...], preferred_element_type=F32)
    o_ref[...] = h_ref[...] + mod_ref[gate_row:gate_row + 1, :] * (_rms(y) * g_ref[...])


def out_proj(lay, acts, w, h, mod, g, *, ssd_extra=None, gate_row=2, nblk=None, name="out_proj"):
    tm, D = lay.tm, h.shape[1]
    K = w.shape[0]
    nblk = lay.nblk if nblk is None else nblk
    row = lambda width: pl.BlockSpec((tm, width), lambda i: (i, 0))
    in_specs = [row(a.shape[1]) for a in acts]
    args = list(acts)
    if ssd_extra is not None:
        dskip, gn = ssd_extra
        in_specs += [pl.BlockSpec((1, K), lambda i: (0, 0))] * 2
        args += [dskip, gn]
    in_specs += [_resident((K, D)), row(D),
                 pl.BlockSpec((None, N_MOD, D), lambda i: (lay.mod_idx(i), 0, 0)),
                 pl.BlockSpec((1, D), lambda i: (0, 0))]
    args += [w, h, mod, g.reshape(1, D)]
    kern = functools.partial(_out_proj_kernel, ssd=ssd_extra is not None, gate_row=gate_row)
    return pl.pallas_call(
        kern, grid=(nblk,), in_specs=in_specs, out_specs=row(D),
        out_shape=jax.ShapeDtypeStruct((nblk * tm, D), F32),
        compiler_params=_cparams(("parallel",)), name=name,
    )(*args)


def _ffn_kernel(h_ref, mod_ref, g_ref, w_in_ref, w_out_ref, o_ref, act_ref, *, hidden, cw):
    h = h_ref[...]
    u = _rms(h) * g_ref[0:1, :]
    ub = (u * (1.0 + mod_ref[4:5, :]) + mod_ref[3:4, :]).astype(BF16)
    for c0 in range(0, hidden, cw):
        gate = jnp.dot(ub, w_in_ref[:, c0:c0 + cw], preferred_element_type=F32)
        up = jnp.dot(ub, w_in_ref[:, hidden + c0:hidden + c0 + cw], preferred_element_type=F32)
        act_ref[:, c0:c0 + cw] = (_silu(gate) * up).astype(BF16)
    y = jnp.dot(act_ref[...], w_out_ref[...], preferred_element_type=F32)
    o_ref[...] = h + mod_ref[5:6, :] * (_rms(y) * g_ref[1:2, :])


def ffn(lay, h, mod, g_pair, w_in, w_out, *, nblk=None, name="ffn"):
    tm, D = lay.tm, h.shape[1]
    hidden = w_out.shape[0]
    nblk = lay.nblk if nblk is None else nblk
    cw = 256
    assert hidden % cw == 0
    kern = functools.partial(_ffn_kernel, hidden=hidden, cw=cw)
    return pl.pallas_call(
        kern, grid=(nblk,),
        in_specs=[pl.BlockSpec((tm, D), lambda i: (i, 0)),
                  pl.BlockSpec((None, N_MOD, D), lambda i: (lay.mod_idx(i), 0, 0)),
                  pl.BlockSpec((2, D), lambda i: (0, 0)),
                  _resident((D, 2 * hidden)), _resident((hidden, D))],
        out_specs=pl.BlockSpec((tm, D), lambda i: (i, 0)),
        out_shape=jax.ShapeDtypeStruct((nblk * tm, D), F32),
        scratch_shapes=[pltpu.VMEM((tm, hidden), BF16)],
        compiler_params=_cparams(("parallel",)), name=name,
    )(h, mod, g_pair, w_in, w_out)


def _da_attn_kernel(lam_ref, subln_ref, q_ref, k_ref, vt_ref, o_ref, acc0, acc1, s00, s01, s10, s11,
                    p00, p01, p10, p11, *, lam_init, tk, n_chunks):
    tq = q_ref.shape[0]
    accs = (acc0, acc1)
    s_bufs = ((s00, s01), (s10, s11))
    p_bufs = ((p00, p01), (p10, p11))
    qt = q_ref[...].astype(F32).T
    row = lax.broadcasted_iota(jnp.int32, qt.shape, 0)
    qts = (jnp.where(row < HEAD_DIM, qt, 0.0).astype(BF16), jnp.where(row >= HEAD_DIM, qt, 0.0).astype(BF16))

    def scores(c, slot):
        kc = k_ref[pl.ds(pl.multiple_of(c * tk, tk), tk), :]
        for mp in range(2):
            s_bufs[slot][mp][...] = jnp.dot(kc, qts[mp], preferred_element_type=F32)

    def accumulate(c, slot, alphas):
        vt = vt_ref[:, pl.ds(pl.multiple_of(c * tk, tk), tk)]
        for mp in range(2):
            accs[mp][...] = alphas[mp] * accs[mp][...] + jnp.dot(vt, p_bufs[slot][mp][...],
                                                                 preferred_element_type=F32)

    def softmax(slot, stats):
        new, alphas = [], []
        for mp in range(2):
            m, l = stats[2 * mp:2 * mp + 2]
            s = s_bufs[slot][mp][...]
            m_new = jnp.maximum(m, jnp.max(s, axis=0, keepdims=True))
            alpha = jnp.exp2(m - m_new)
            p = jnp.exp2(s - m_new)
            p_bufs[slot][mp][...] = p.astype(BF16)
            new += [m_new, alpha * l + jnp.sum(p, axis=0, keepdims=True)]
            alphas.append(alpha)
        return tuple(new), tuple(alphas)

    for mp in range(2):
        accs[mp][...] = jnp.zeros_like(accs[mp])
        p_bufs[1][mp][...] = jnp.zeros_like(p_bufs[1][mp])
    scores(0, 0)
    stats = (jnp.full((1, tq), NEG_INF, F32), jnp.zeros((1, tq), F32)) * 2
    ones = (jnp.ones((1, tq), F32),) * 2
    assert n_chunks % 2 == 1 and k_ref.shape[0] == n_chunks * tk

    n_pairs = (n_chunks - 1) // 2
    unroll = math.gcd(n_pairs, DA_UNROLL) if n_pairs else 1

    def pairs(t, carry):
        stats, alphas1 = carry
        for u in range(unroll):
            c = 2 * (t * unroll + u)
            scores(c + 1, 1)
            accumulate(jnp.maximum(c - 1, 0), 1, alphas1)
            stats, alphas0 = softmax(0, stats)
            scores(c + 2, 0)
            accumulate(c, 0, alphas0)
            stats, alphas1 = softmax(1, stats)
        return stats, alphas1

    stats, alphas1 = lax.fori_loop(0, n_pairs // unroll, pairs, (stats, ones))
    last = n_chunks - 1
    accumulate(jnp.maximum(last - 1, 0), 1, alphas1)
    stats, alphas0 = softmax(0, stats)
    accumulate(last, 0, alphas0)

    lp = lam_ref[...]
    lam = (jnp.exp(jnp.sum(lp[0:1] * lp[1:2], axis=-1, keepdims=True))
           - jnp.exp(jnp.sum(lp[2:3] * lp[3:4], axis=-1, keepdims=True)) + lam_init)
    ot = acc0[...] / stats[1] - lam * (acc1[...] / stats[3])
    o = _rms(ot.T) * subln_ref[...] * (1.0 - lam_init)
    o_ref[...] = o.astype(o_ref.dtype)


def da_attention(B, S, Lc, q_src, k_all, vt_all, lam_p, subln, lam_init, want_ctx):
    T = q_src.shape[0]
    width = DA_HEADS * 2 * HEAD_DIM
    n_keys = Lc + S
    small = (jnp.asarray(lam_p), subln.reshape(1, 2 * HEAD_DIM))
    small_specs = [pl.BlockSpec((4, HEAD_DIM), lambda b, h, i: (0, 0)),
                   pl.BlockSpec((1, 2 * HEAD_DIM), lambda b, h, i: (0, 0))]

    def call(tq, tk, n_rows_k, k_blocks_per_batch, nq, q_block0, prev):
        n_chunks = n_rows_k // tk
        kern = functools.partial(_da_attn_kernel, lam_init=lam_init, tk=tk, n_chunks=n_chunks)
        q_idx = lambda b, h, i: (q_block0 + b * nq + i, h)
        in_specs = small_specs + [pl.BlockSpec((tq, LANES), q_idx),
                                  pl.BlockSpec((n_rows_k, LANES), lambda b, h, i: (b * k_blocks_per_batch, h)),
                                  pl.BlockSpec((LANES, n_rows_k), lambda b, h, i: (h, b * k_blocks_per_batch))]
        args = small + (q_src, k_all, vt_all)
        aliases = {}
        body = kern
        if prev is not None:
            in_specs.append(pl.BlockSpec(memory_space=pl.ANY))
            args += (prev,)
            aliases = {len(args) - 1: 0}
            body = lambda *refs: kern(*refs[:5], *refs[6:])
        return pl.pallas_call(
            body, grid=(B, DA_HEADS, nq), in_specs=in_specs,
            out_specs=pl.BlockSpec((tq, LANES), q_idx),
            out_shape=jax.ShapeDtypeStruct((T, width), BF16),
            scratch_shapes=([pltpu.VMEM((LANES, tq), F32)] * 2 + [pltpu.VMEM((tk, tq), F32)] * 4
                            + [pltpu.VMEM((tk, tq), BF16)] * 4),
            input_output_aliases=aliases,
            compiler_params=_cparams(("parallel", "parallel", "arbitrary")),
            name="da_attention" if prev is None else "da_attention_ctx",
        )(*args)

    tq = min(DA_TQ, S)
    tk = next(t for t in DA_TK if n_keys % t == 0 and (n_keys // t) % 2 == 1)
    o = call(tq, tk, n_keys, 1, S // tq, 0, None)
    if want_ctx:
        assert n_keys % Lc == 0
        o = call(Lc, Lc, Lc, n_keys // Lc, 1, B * S // Lc, o)
    return o


def _sw_attn_kernel(sink_ref, q_ref, kc_ref, vc_ref, kp_ref, kx_ref, kn_ref, vp_ref, vx_ref, vn_ref,
                    o_ref, *, n_lat_q, nb):
    tq = q_ref.shape[0]
    Lc = kc_ref.shape[0]
    i = pl.program_id(1)
    is_lat = i < n_lat_q
    n = i % nb
    krel = lax.broadcasted_iota(jnp.int32, (3 * SW_BLOCK, tq), 0) - SW_BLOCK
    qrow = lax.broadcasted_iota(jnp.int32, (3 * SW_BLOCK, tq), 1)
    kabs = krel + n * SW_BLOCK
    valid = (jnp.abs(qrow - krel) <= SW_WINDOW) & (kabs >= 0) & (kabs < nb * SW_BLOCK) & is_lat
    valid = jnp.concatenate([valid] * SW_GROUP, axis=1)
    row = lax.broadcasted_iota(jnp.int32, (LANES, SW_GROUP * tq), 0)
    lo = row < HEAD_DIM
    for j in range(SW_KV_HEADS // 2):
        ksl = slice(j * LANES, (j + 1) * LANES)
        keys = jnp.concatenate([kc_ref[:, ksl], kp_ref[:, ksl], kx_ref[:, ksl], kn_ref[:, ksl]], axis=0)
        vals = jnp.concatenate([vc_ref[:, ksl], vp_ref[:, ksl], vx_ref[:, ksl], vn_ref[:, ksl]], axis=0)
        vals_t = vals.astype(F32).T.astype(BF16)
        qt = jnp.concatenate(
            [q_ref[:, (j * SW_GROUP + g) * LANES:(j * SW_GROUP + g + 1) * LANES].astype(F32).T
             for g in range(SW_GROUP)], axis=1)
        halves = []
        for e in range(2):
            sink = sink_ref[2 * j + e:2 * j + e + 1, :]
            qz = jnp.where(lo if e == 0 else ~lo, qt, 0.0).astype(BF16)
            s = jnp.dot(keys, qz, preferred_element_type=F32)
            s_ctx = s[:Lc]
            s_loc = jnp.where(valid, s[Lc:], NEG_INF)
            m = jnp.maximum(jnp.maximum(jnp.max(s_ctx, axis=0, keepdims=True),
                                        jnp.max(s_loc, axis=0, keepdims=True)), sink)
            p_ctx = jnp.exp(s_ctx - m)
            p_loc = jnp.exp(s_loc - m)
            denom = (jnp.sum(p_ctx, axis=0, keepdims=True) + jnp.sum(p_loc, axis=0, keepdims=True)
                     + jnp.exp(sink - m))
            p = jnp.concatenate([p_ctx, p_loc], axis=0).astype(BF16)
            halves.append(jnp.dot(vals_t, p, preferred_element_type=F32) / denom)
        ot = jnp.where(lo, halves[0], halves[1])
        for g in range(SW_GROUP):
            slab = j * SW_GROUP + g
            o_ref[:, slab * LANES:(slab + 1) * LANES] = ot[:, g * tq:(g + 1) * tq].T.astype(o_ref.dtype)


def sw_attention(B, S, Lc, qkv, sink, want_ctx):
    T = qkv.shape[0]
    qw = SW_HEADS * HEAD_DIM
    kvw = SW_KV_HEADS * HEAD_DIM
    tq = SW_BLOCK
    nb = S // SW_BLOCK
    nqc = Lc // tq
    nq = nb + (nqc if want_ctx else 0)
    kcol, vcol = qw // kvw, qw // kvw + 1

    def q_idx(b, i):
        return (jnp.where(i < nb, b * nb + i, B * nb + b * nqc + (i - nb)), 0)

    def band(col, off):
        def idx(b, i):
            n = jnp.clip(jnp.where(i < nb, i, 0) + off, 0, nb - 1)
            return (b * nb + n, col)
        return pl.BlockSpec((SW_BLOCK, kvw), idx)

    sink_lanes = jnp.repeat(sink.astype(F32).reshape(SW_KV_HEADS, SW_GROUP), tq, axis=1)
    kern = functools.partial(_sw_attn_kernel, n_lat_q=nb, nb=nb)
    return pl.pallas_call(
        kern, grid=(B, nq),
        in_specs=[pl.BlockSpec((SW_KV_HEADS, SW_GROUP * tq), lambda b, i: (0, 0)),
                  pl.BlockSpec((tq, qw), q_idx),
                  pl.BlockSpec((Lc, kvw), lambda b, i: (B * S // Lc + b, kcol)),
                  pl.BlockSpec((Lc, kvw), lambda b, i: (B * S // Lc + b, vcol)),
                  band(kcol, -1), band(kcol, 0), band(kcol, 1),
                  band(vcol, -1), band(vcol, 0), band(vcol, 1)],
        out_specs=pl.BlockSpec((tq, qw), q_idx),
        out_shape=jax.ShapeDtypeStruct((T, qw), BF16),
        compiler_params=_cparams(("parallel", "arbitrary")), name="sw_attention",
    )(sink_lanes, qkv, qkv, qkv, qkv, qkv, qkv, qkv, qkv, qkv)


def _sw_head_perm():
    cols = []
    for j in range(SW_KV_HEADS // 2):
        for g in range(SW_GROUP):
            for e in range(2):
                head = (2 * j + e) * SW_GROUP + g
                cols += list(range(head * HEAD_DIM, (head + 1) * HEAD_DIM))
    return np.asarray(cols, dtype=np.int32)


def _ssd_conv_kernel(prev_ref, cur_ref, next_ref, w_ref, b_ref, o_ref, ext_ref, *, seg_lat, seg_ctx, n_lat_blk):
    tc = cur_ref.shape[0]
    i = pl.program_id(0)
    row0 = i * tc
    is_lat = i < n_lat_blk
    seg = jnp.where(is_lat, seg_lat, seg_ctx)
    rel = jnp.where(is_lat, row0, row0 - n_lat_blk * tc) % seg
    keep_prev = (rel != 0).astype(F32)
    keep_next = (rel + tc != seg).astype(F32)
    ext_ref[0:CONV_HALO, :] = prev_ref[...].astype(F32) * keep_prev
    ext_ref[CONV_HALO:CONV_HALO + tc, :] = cur_ref[...].astype(F32)
    ext_ref[CONV_HALO + tc:, :] = next_ref[...].astype(F32) * keep_next
    acc = jnp.zeros(o_ref.shape, F32) + b_ref[...]
    for k in range(SSD_CONV):
        off = CONV_HALO - SSD_CONV // 2 + k
        acc = acc + ext_ref[off:off + tc, :] * w_ref[k:k + 1, :]
    o_ref[...] = _silu(acc).astype(o_ref.dtype)


def ssd_conv(B, S, Lc, xbc, conv_w, conv_b):
    T, C = xbc.shape
    tc = min(256, Lc)
    cwid = next(w for w in (1536, 1024, 512) if C % w == 0)
    nblk = T // tc
    hb = tc // CONV_HALO
    nh = T // CONV_HALO
    kern = functools.partial(_ssd_conv_kernel, seg_lat=S, seg_ctx=Lc, n_lat_blk=B * S // tc)
    return pl.pallas_call(
        kern, grid=(nblk, C // cwid),
        in_specs=[pl.BlockSpec((CONV_HALO, cwid), lambda i, j: (jnp.maximum(i * hb - 1, 0), j)),
                  pl.BlockSpec((tc, cwid), lambda i, j: (i, j)),
                  pl.BlockSpec((CONV_HALO, cwid), lambda i, j: (jnp.minimum((i + 1) * hb, nh - 1), j)),
                  pl.BlockSpec((SSD_CONV, cwid), lambda i, j: (0, j)),
                  pl.BlockSpec((1, cwid), lambda i, j: (0, j))],
        out_specs=pl.BlockSpec((tc, cwid), lambda i, j: (i, j)),
        out_shape=jax.ShapeDtypeStruct((T, C), BF16),
        scratch_shapes=[pltpu.VMEM((tc + 2 * CONV_HALO, cwid), F32)],
        compiler_params=_cparams(("parallel", "parallel")), name="ssd_conv",
    )(xbc, xbc, xbc, conv_w, conv_b.reshape(1, C))


def _split3(t):
    hi = t.astype(BF16)
    r1 = t - hi.astype(F32)
    mid = r1.astype(BF16)
    lo = (r1 - mid.astype(F32)).astype(BF16)
    return hi, mid, lo


def _dot01(t, ones_mat):
    out = None
    for part in _split3(t):
        term = jnp.dot(part, ones_mat, preferred_element_type=F32)
        out = term if out is None else out + term
    return out


def _dot_left01(ones_mat, t):
    out = None
    for part in _split3(t):
        term = jnp.dot(ones_mat, part, preferred_element_type=F32)
        out = term if out is None else out + term
    return out


def _ssd_scan_kernel(xf_ref, xb_ref, dtf_ref, dtb_ref, aneg_ref, bias_ref, ef_ref, eb_ref,
                     yf_ref, yb_ref, sf_ref, sb_ref, *, inner, n_heads):
    @pl.when(pl.program_id(1) == 0)
    def _():
        sf_ref[...] = jnp.zeros_like(sf_ref)
        sb_ref[...] = jnp.zeros_like(sb_ref)

    for reverse, x_ref, dt_ref, e_ref, y_ref, s_ref in (
            (False, xf_ref, dtf_ref, ef_ref, yf_ref, sf_ref),
            (True, xb_ref, dtb_ref, eb_ref, yb_ref, sb_ref)):
        _ssd_chunk(x_ref, dt_ref, aneg_ref, bias_ref, e_ref, y_ref, s_ref,
                   reverse=reverse, inner=inner, n_heads=n_heads)


def _ssd_chunk(x_ref, dt_ref, aneg_ref, bias_ref, e_ref, y_ref, s_ref, *, reverse, inner, n_heads):
    L = x_ref.shape[0]
    gn = SSD_GROUPS * SSD_STATE
    hpg = n_heads // SSD_GROUPS
    gw = hpg * SSD_HEAD_DIM
    head0 = n_heads if reverse else 0

    z = dt_ref[...] + bias_ref[...]
    dt = jnp.maximum(z, 0.0) + jnp.log1p(jnp.exp(-jnp.abs(z)))
    a = dt * (-jnp.exp(aneg_ref[...]))
    r = lax.broadcasted_iota(jnp.int32, (L, L), 0)
    c = lax.broadcasted_iota(jnp.int32, (L, L), 1)
    causal = (c >= r) if reverse else (c <= r)
    tri = jnp.where(causal, 1.0, 0.0).astype(BF16)
    acs = _dot_left01(tri, a)
    a_tot = jnp.sum(a, axis=0, keepdims=True)
    dte = jnp.exp(a_tot - acs)
    eacs = jnp.exp(acs)
    acs_t = acs.T

    dt_t = dt.T

    pk = 2 * SUBLANES
    eacs_hi, eacs_mid, _ = _split3(eacs)
    cdec_parts = [jnp.broadcast_to(part, (pk, LANES)) for part in _split3(jnp.exp(a_tot))]
    stack = jnp.concatenate([(dt * dte).astype(BF16), eacs_hi, eacs_mid] + cdec_parts, axis=0)
    ex = jnp.dot(stack, e_ref[...], preferred_element_type=F32)
    dtdte_e = ex[0:L]
    eacs_e = ex[L:2 * L] + ex[2 * L:3 * L]
    cdec_e = ex[3 * L:3 * L + 1] + ex[3 * L + pk:3 * L + pk + 1] + ex[3 * L + 2 * pk:3 * L + 2 * pk + 1]

    xw = (x_ref[:, :inner].astype(F32) * dtdte_e).astype(BF16)
    lane = lax.broadcasted_iota(jnp.int32, (L, LANES), 1)
    lo = lane < SSD_HEAD_DIM
    nt = (((1,), (1,)), ((), ()))
    tn = (((0,), (0,)), ((), ()))
    for g in range(SSD_GROUPS):
        Bg = x_ref[:, inner + g * SSD_STATE:inner + (g + 1) * SSD_STATE]
        Cg = x_ref[:, inner + gn + g * SSD_STATE:inner + gn + (g + 1) * SSD_STATE]
        gs = slice(g * gw, (g + 1) * gw)
        cb = lax.dot_general(Cg, Bg, nt, preferred_element_type=F32)
        s_prev = s_ref[:, gs]
        y_off = jnp.dot(Cg, s_prev.astype(BF16), preferred_element_type=F32) * eacs_e[:, gs]
        for k in range(hpg // 2):
            sl = slice(g * gw + k * LANES, g * gw + (k + 1) * LANES)
            halves = []
            for eh in range(2):
                col = head0 + g * hpg + 2 * k + eh
                seg = acs[:, col:col + 1] - acs_t[col:col + 1, :]
                w = cb * jnp.exp(jnp.where(causal, seg, NEG_INF)) * dt_t[col:col + 1, :]
                halves.append(jnp.dot(w.astype(BF16), x_ref[:, sl], preferred_element_type=F32))
            y = jnp.where(lo, halves[0], halves[1]) + y_off[:, k * LANES:(k + 1) * LANES]
            y_ref[:, sl] = y.astype(y_ref.dtype)
        s_ref[:, gs] = s_prev * cdec_e[:, gs] + lax.dot_general(Bg, xw[:, gs], tn, preferred_element_type=F32)


def ssd_scan(B, S, Lc, xbc, dt_raw, a_log_pad, dt_bias_pad, inner, n_heads):
    T, C = xbc.shape
    L = SSD_CHUNK
    nl, nc = S // L, Lc // L
    nsteps = nl + nc

    def fwd_idx(b, i):
        return (jnp.where(i < nc, B * nl + b * nc + i, b * nl + (i - nc)), 0)

    def bwd_idx(b, i):
        return (jnp.where(i < nc, B * nl + b * nc + (nc - 1 - i), b * nl + (nl - 1 - (i - nc))), 0)

    hd = np.arange(inner) // SSD_HEAD_DIM
    ef = (np.arange(LANES)[:, None] == hd[None, :]).astype(np.float32)
    eb = (np.arange(LANES)[:, None] == (hd[None, :] + n_heads)).astype(np.float32)
    kern = functools.partial(_ssd_scan_kernel, inner=inner, n_heads=n_heads)
    const = lambda shape: pl.BlockSpec(shape, lambda b, i: (0, 0))
    return pl.pallas_call(
        kern, grid=(B, nsteps),
        in_specs=[pl.BlockSpec((L, C), fwd_idx), pl.BlockSpec((L, C), bwd_idx),
                  pl.BlockSpec((L, LANES), fwd_idx), pl.BlockSpec((L, LANES), bwd_idx),
                  const((1, LANES)), const((1, LANES)), const((LANES, inner)), const((LANES, inner))],
        out_specs=[pl.BlockSpec((L, inner), fwd_idx), pl.BlockSpec((L, inner), bwd_idx)],
        out_shape=[jax.ShapeDtypeStruct((T, inner), BF16)] * 2,
        scratch_shapes=[pltpu.VMEM((SSD_STATE, inner), F32)] * 2,
        compiler_params=_cparams(("parallel", "arbitrary")), name="ssd_scan",
    )(xbc, xbc, dt_raw, dt_raw, a_log_pad, dt_bias_pad, jnp.asarray(ef, BF16), jnp.asarray(eb, BF16))


def _rope_tables(S, tm):
    n_rows = S // GRID_W
    rows = jnp.repeat(jnp.arange(n_rows), GRID_W)
    cols = jnp.tile(jnp.arange(GRID_W), n_rows)
    n_freq = HEAD_DIM // 4
    freqs = ROPE_THETA ** (-jnp.arange(n_freq, dtype=F32) / n_freq)
    ang = jnp.concatenate([rows[:, None].astype(F32) * freqs, cols[:, None].astype(F32) * freqs], axis=-1)
    cos, sin = jnp.cos(ang), jnp.sin(ang)
    cos_t = jnp.concatenate([cos, cos, cos, cos], axis=-1)
    sin_t = jnp.concatenate([-sin, sin, -sin, sin], axis=-1)
    cos_t = jnp.concatenate([cos_t, jnp.ones((tm, LANES), F32)], axis=0)
    sin_t = jnp.concatenate([sin_t, jnp.zeros((tm, LANES), F32)], axis=0)
    return cos_t, sin_t


def kernel(x, c, ctx, c_ctx, ada_w, ada_b, norm_g, ffn_w_in, ffn_w_out, da_w_qkv, da_w_o, da_lambda, da_subln,
           sw_w_qkv, sw_w_o, sw_sink, ssd_w_in, ssd_conv_w, ssd_conv_b, ssd_a_log, ssd_dt_bias, ssd_d_skip,
           ssd_norm, ssd_w_out):
    B, S, D = x.shape
    Lc = ctx.shape[1]
    depth = ada_w.shape[0]
    tm = min(512, math.gcd(S, B * Lc))
    lay = Layout(B, S, Lc, tm)

    n_cond = -(-(B + 1) // SUBLANES) * SUBLANES
    cond = jnp.concatenate([c, c_ctx[None, :], jnp.zeros((n_cond - B - 1, D), F32)], axis=0)
    mods = ada_modulation(cond, ada_w, ada_b)
    rope = _rope_tables(S, tm)

    h = jnp.concatenate([x.reshape(B * S, D), ctx.reshape(B * Lc, D)], axis=0)
    da_width = DA_HEADS * 2 * HEAD_DIM
    sw_qw = SW_HEADS * HEAD_DIM
    sw_kvw = SW_KV_HEADS * HEAD_DIM
    perm = _sw_head_perm()

    for i in range(depth):
        want_ctx = i < depth - 1
        j = i // N_MIXERS
        kind = i % N_MIXERS
        mod = mods[i]
        g = norm_g[i]
        nblk = None if want_ctx else lay.nlat
        if kind == 0:
            lam_init = 0.8 - 0.6 * math.exp(-0.3 * i)
            w = da_w_qkv[j]
            q_scale = HEAD_DIM ** -0.5 * math.log2(math.e)
            w = jnp.concatenate([w[:, :da_width] * q_scale, w[:, da_width:]], axis=1).astype(BF16)
            (qkv,) = in_proj(lay, h, mod, g[0], w, [(0, 3 * da_width)], [BF16], rope=rope,
                             rope_cols=2 * da_width, name="da_in_proj")
            kv = jnp.concatenate([qkv[B * S:, da_width:].reshape(B, Lc, 2 * da_width),
                                  qkv[:B * S, da_width:].reshape(B, S, 2 * da_width)], axis=1)
            k_all = kv[:, :, :da_width].reshape(B * (Lc + S), da_width)
            vt_all = kv[:, :, da_width:].reshape(B * (Lc + S), da_width).T
            o = da_attention(B, S, Lc, qkv, k_all, vt_all, da_lambda[j], da_subln[j], lam_init, want_ctx)
            h = out_proj(lay, [o], da_w_o[j].astype(BF16), h, mod, g[1], nblk=nblk, name="da_out_proj")
        elif kind == 1:
            w = sw_w_qkv[j]
            w = jnp.concatenate([w[:, :sw_qw][:, perm] * HEAD_DIM ** -0.5, w[:, sw_qw:]], axis=1).astype(BF16)
            (qkv,) = in_proj(lay, h, mod, g[0], w, [(0, sw_qw + 2 * sw_kvw)], [BF16], rope=rope,
                             rope_cols=sw_qw + sw_kvw, cw=256, name="sw_in_proj")
            o = sw_attention(B, S, Lc, qkv, sw_sink[j], want_ctx)
            h = out_proj(lay, [o], sw_w_o[j][perm, :].astype(BF16), h, mod, g[1], nblk=nblk, name="sw_out_proj")
        else:
            n_heads = ssd_a_log.shape[2]
            inner = n_heads * SSD_HEAD_DIM
            conv_ch = inner + 2 * SSD_GROUPS * SSD_STATE
            w = ssd_w_in[j]
            w = jnp.pad(w, ((0, 0), (0, LANES - 2 * n_heads))).astype(BF16)
            z, xbc, dt_raw = in_proj(lay, h, mod, g[0], w,
                                     [(0, inner), (inner, conv_ch), (inner + conv_ch, LANES)],
                                     [BF16, BF16, F32], name="ssd_in_proj")
            xbc = ssd_conv(B, S, Lc, xbc, ssd_conv_w[j], ssd_conv_b[j])
            pad = lambda t: jnp.pad(t.reshape(1, 2 * n_heads), ((0, 0), (0, LANES - 2 * n_heads)))
            yf, yb = ssd_scan(B, S, Lc, xbc, dt_raw, pad(ssd_a_log[j]), pad(ssd_dt_bias[j]), inner, n_heads)
            dskip = jnp.repeat(ssd_d_skip[j], SSD_HEAD_DIM).reshape(1, inner)
            h = out_proj(lay, [yf, yb, xbc, z], ssd_w_out[j].astype(BF16), h, mod, g[1],
                         ssd_extra=(dskip, ssd_norm[j].reshape(1, inner)), nblk=nblk, name="ssd_out_proj")
        h = ffn(lay, h, mod, g[2:4], ffn_w_in[i].astype(BF16), ffn_w_out[i].astype(BF16), nblk=nblk)
    return h[:B * S].reshape(B, S, D)
```

```python
import functools
import math

import numpy as np
import jax
import jax.numpy as jnp
from jax import lax
from jax.experimental import pallas as pl
from jax.experimental.pallas import tpu as pltpu

F32 = jnp.float32
BF16 = jnp.bfloat16

NORM_EPS = 1e-6
ROPE_THETA = 10000.0
NEG_INF = -1e30
GRID_W = 64
N_MOD = 6
N_MIXERS = 3

LANES = 128
SUBLANES = 8
VMEM_LIMIT = 56 * 1024 * 1024

HEAD_DIM = 64
DA_HEADS = 8
SW_HEADS = 16
SW_KV_HEADS = 4
SW_GROUP = SW_HEADS // SW_KV_HEADS
SW_WINDOW = 128
SW_BLOCK = 128
SSD_HEAD_DIM = 64
SSD_GROUPS = 4
SSD_STATE = 128
SSD_CONV = 5
SSD_CHUNK = 128
DA_TQ = 256
DA_TK = (256,)
DA_UNROLL = 8
CONV_HALO = 8


def _cparams(sem):
    return pltpu.CompilerParams(dimension_semantics=sem, vmem_limit_bytes=VMEM_LIMIT)


def _resident(shape):
    nd = len(shape)
    return pl.BlockSpec(shape, lambda *_: (0,) * nd, pipeline_mode=pl.Buffered(1))


def _rms(t):
    return t * lax.rsqrt(jnp.mean(t * t, axis=-1, keepdims=True) + NORM_EPS)


def _silu(t):
    return t * jax.nn.sigmoid(t)


def _ada_kernel(c_ref, w_ref, b_ref, o_ref):
    act = _silu(c_ref[...])
    o_ref[...] = jnp.dot(act, w_ref[...], precision=lax.Precision.HIGHEST,
                         preferred_element_type=F32) + b_ref[...]


def ada_modulation(cond, ada_w, ada_b):
    L, D, N = ada_w.shape
    R = cond.shape[0]
    tn = 1536
    out = pl.pallas_call(
        _ada_kernel,
        grid=(L, N // tn),
        in_specs=[pl.BlockSpec((R, D), lambda l, j: (0, 0)),
                  pl.BlockSpec((None, D, tn), lambda l, j: (l, 0, j)),
                  pl.BlockSpec((None, 1, tn), lambda l, j: (l, 0, j))],
        out_specs=pl.BlockSpec((None, R, tn), lambda l, j: (l, 0, j)),
        out_shape=jax.ShapeDtypeStruct((L, R, N), F32),
        compiler_params=_cparams(("arbitrary", "arbitrary")),
        name="ada_modulation",
    )(cond, ada_w, ada_b.reshape(L, 1, N))
    return out.reshape(L, R, N_MOD, D)


class Layout:
    def __init__(self, B, S, Lc, tm):
        assert S % tm == 0 and (B * Lc) % tm == 0
        self.B, self.S, self.Lc, self.tm = B, S, Lc, tm
        self.T = B * S + B * Lc
        self.npb = S // tm
        self.nlat = B * self.npb
        self.nblk = self.T // tm

    def mod_idx(self, i):
        return jnp.minimum(i // self.npb, self.B)

    def pos_idx(self, i):
        return jnp.where(i < self.nlat, i % self.npb, self.npb)

    def kv_block(self, i):
        per_batch = (self.S + self.Lc) // self.tm
        ncb = self.Lc // self.tm
        ci = i - self.nlat
        return jnp.where(i < self.nlat, (i // self.npb) * per_batch + i % self.npb,
                         (ci // ncb) * per_batch + self.npb + ci % ncb)


def _rope_slab(y, cos, sin_signed, lane):
    fwd = pltpu.roll(y, LANES - HEAD_DIM // 2, 1)
    bwd = pltpu.roll(y, HEAD_DIM // 2, 1)
    rot = jnp.where(lane % HEAD_DIM < HEAD_DIM // 2, fwd, bwd)
    return y * cos + rot * sin_signed


def _in_proj_kernel(*refs, segs, transposed, rope_cols, shift_row, scale_row, cw):
    h_ref, mod_ref, g_ref, w_ref = refs[:4]
    k = 4
    if rope_cols:
        cos_ref, sin_ref = refs[4:6]
        k = 6
    out_refs = refs[k:]
    h = h_ref[...]
    u = _rms(h) * g_ref[...]
    u = u * (1.0 + mod_ref[scale_row:scale_row + 1, :]) + mod_ref[shift_row:shift_row + 1, :]
    ub = u.astype(BF16)
    if rope_cols:
        cos = cos_ref[...]
        sin = sin_ref[...]
        lane = lax.broadcasted_iota(jnp.int32, cos.shape, 1)
    for o_ref, (col0, width), tr in zip(out_refs, segs, transposed):
        for c0 in range(0, width, cw):
            w_c = min(cw, width - c0)
            y = jnp.dot(ub, w_ref[:, col0 + c0:col0 + c0 + w_c], preferred_element_type=F32)
            if col0 + c0 < rope_cols:
                assert col0 + c0 + w_c <= rope_cols and w_c % LANES == 0
                y = jnp.concatenate(
                    [_rope_slab(y[:, s:s + LANES], cos, sin, lane) for s in range(0, w_c, LANES)], axis=1)
            if tr:
                o_ref[c0:c0 + w_c, :] = y.T.astype(o_ref.dtype)
            else:
                o_ref[:, c0:c0 + w_c] = y.astype(o_ref.dtype)


def in_proj(lay, h, mod, g, w, segs, out_dtypes, *, rope=None, rope_cols=0, shift_row=0, scale_row=1,
            cw=512, out_rows=None, name="in_proj"):
    tm, D = lay.tm, h.shape[1]
    N = w.shape[1]
    out_rows = out_rows or [None] * len(segs)
    out_rows = [r or (lay.T, lambda i: i, False) for r in out_rows]
    in_specs = [pl.BlockSpec((tm, D), lambda i: (i, 0)),
                pl.BlockSpec((None, N_MOD, D), lambda i: (lay.mod_idx(i), 0, 0)),
                pl.BlockSpec((1, D), lambda i: (0, 0)),
                _resident((D, N))]
    args = [h, mod, g.reshape(1, D), w]
    if rope_cols:
        cos, sin = rope
        in_specs += [pl.BlockSpec((tm, LANES), lambda i: (lay.pos_idx(i), 0))] * 2
        args += [cos, sin]
    out_specs, out_shape = [], []
    for (_, width), dt, (n_rows, blk, tr) in zip(segs, out_dtypes, out_rows):
        if tr:
            out_specs.append(pl.BlockSpec((width, tm), lambda i, blk=blk: (0, blk(i))))
            out_shape.append(jax.ShapeDtypeStruct((width, n_rows), dt))
        else:
            out_specs.append(pl.BlockSpec((tm, width), lambda i, blk=blk: (blk(i), 0)))
            out_shape.append(jax.ShapeDtypeStruct((n_rows, width), dt))
    kern = functools.partial(_in_proj_kernel, segs=tuple(segs), transposed=tuple(r[2] for r in out_rows),
                             rope_cols=rope_cols, shift_row=shift_row, scale_row=scale_row, cw=cw)
    return pl.pallas_call(
        kern, grid=(lay.nblk,), in_specs=in_specs, out_specs=out_specs, out_shape=out_shape,
        compiler_params=_cparams(("parallel",)), name=name,
    )(*args)


def _out_proj_kernel(*refs, ssd, gate_row):
    if ssd:
        yf_ref, yb_ref, xbc_ref, z_ref, dskip_ref, gn_ref, w_ref, h_ref, mod_ref, g_ref, o_ref = refs
        inner = yf_ref.shape[1]
        gw = inner // SSD_GROUPS
        y = (yf_ref[...].astype(F32) + yb_ref[...].astype(F32)
             + xbc_ref[:, :inner].astype(F32) * dskip_ref[...])
        y = y * _silu(z_ref[...].astype(F32))
        y = jnp.concatenate([_rms(y[:, k * gw:(k + 1) * gw]) for k in range(SSD_GROUPS)], axis=1)
        a = (y * gn_ref[...]).astype(BF16)
    else:
        a_ref, w_ref, h_ref, mod_ref, g_ref, o_ref = refs
        a = a_ref[...]
    y = jnp.dot(a, w_ref[...], preferred_element_type=F32)
    o_ref[...] = h_ref[...] + mod_ref[gate_row:gate_row + 1, :] * (_rms(y) * g_ref[...])


def out_proj(lay, acts, w, h, mod, g, *, ssd_extra=None, gate_row=2, nblk=None, name="out_proj"):
    tm, D = lay.tm, h.shape[1]
    K = w.shape[0]
    nblk = lay.nblk if nblk is None else nblk
    row = lambda width: pl.BlockSpec((tm, width), lambda i: (i, 0))
    in_specs = [row(a.shape[1]) for a in acts]
    args = list(acts)
    if ssd_extra is not None:
        dskip, gn = ssd_extra
        in_specs += [pl.BlockSpec((1, K), lambda i: (0, 0))] * 2
        args += [dskip, gn]
    in_specs += [_resident((K, D)), row(D),
                 pl.BlockSpec((None, N_MOD, D), lambda i: (lay.mod_idx(i), 0, 0)),
                 pl.BlockSpec((1, D), lambda i: (0, 0))]
    args += [w, h, mod, g.reshape(1, D)]
    kern = functools.partial(_out_proj_kernel, ssd=ssd_extra is not None, gate_row=gate_row)
    return pl.pallas_call(
        kern, grid=(nblk,), in_specs=in_specs, out_specs=row(D),
        out_shape=jax.ShapeDtypeStruct((nblk * tm, D), F32),
        compiler_params=_cparams(("parallel",)), name=name,
    )(*args)


def _ffn_kernel(*refs, hidden, cw, with_mixer):
    if with_mixer:
        a_ref, wo_ref, h_ref, mod_ref, g_ref, w_in_ref, w_out_ref, o_ref, act_ref = refs
        y = jnp.dot(a_ref[...], wo_ref[...], preferred_element_type=F32)
        h = h_ref[...] + mod_ref[2:3, :] * (_rms(y) * g_ref[1:2, :])
    else:
        h_ref, mod_ref, g_ref, w_in_ref, w_out_ref, o_ref, act_ref = refs
        h = h_ref[...]
    u = _rms(h) * g_ref[2:3, :]
    ub = (u * (1.0 + mod_ref[4:5, :]) + mod_ref[3:4, :]).astype(BF16)
    for c0 in range(0, hidden, cw):
        gate = jnp.dot(ub, w_in_ref[:, c0:c0 + cw], preferred_element_type=F32)
        up = jnp.dot(ub, w_in_ref[:, hidden + c0:hidden + c0 + cw], preferred_element_type=F32)
        act_ref[:, c0:c0 + cw] = (_silu(gate) * up).astype(BF16)
    y = jnp.dot(act_ref[...], w_out_ref[...], preferred_element_type=F32)
    o_ref[...] = h + mod_ref[5:6, :] * (_rms(y) * g_ref[3:4, :])


def ffn(lay, h, mod, g4, w_in, w_out, *, mixer=None, nblk=None, name="ffn"):
    tm, D = lay.tm, h.shape[1]
    hidden = w_out.shape[0]
    nblk = lay.nblk if nblk is None else nblk
    cw = 256
    assert hidden % cw == 0
    row = lambda width: pl.BlockSpec((tm, width), lambda i: (i, 0))
    in_specs, args = [], []
    if mixer is not None:
        a, w_o = mixer
        in_specs += [row(a.shape[1]), _resident(w_o.shape)]
        args += [a, w_o]
    in_specs += [row(D), pl.BlockSpec((None, N_MOD, D), lambda i: (lay.mod_idx(i), 0, 0)),
                 pl.BlockSpec((4, D), lambda i: (0, 0)),
                 _resident((D, 2 * hidden)), _resident((hidden, D))]
    args += [h, mod, g4, w_in, w_out]
    kern = functools.partial(_ffn_kernel, hidden=hidden, cw=cw, with_mixer=mixer is not None)
    return pl.pallas_call(
        kern, grid=(nblk,), in_specs=in_specs, out_specs=row(D),
        out_shape=jax.ShapeDtypeStruct((nblk * tm, D), F32),
        scratch_shapes=[pltpu.VMEM((tm, hidden), BF16)],
        compiler_params=_cparams(("parallel",)), name=name,
    )(*args)


def _da_attn_kernel(lam_ref, subln_ref, q_ref, k_ref, vt_ref, o_ref, acc0, acc1, s00, s01, s10, s11,
                    p00, p01, p10, p11, *, lam_init, tk, n_chunks):
    tq = q_ref.shape[0]
    accs = (acc0, acc1)
    s_bufs = ((s00, s01), (s10, s11))
    p_bufs = ((p00, p01), (p10, p11))
    qt = q_ref[...].astype(F32).T
    row = lax.broadcasted_iota(jnp.int32, qt.shape, 0)
    qts = (jnp.where(row < HEAD_DIM, qt, 0.0).astype(BF16), jnp.where(row >= HEAD_DIM, qt, 0.0).astype(BF16))

    def scores(c, slot):
        kc = k_ref[pl.ds(pl.multiple_of(c * tk, tk), tk), :]
        for mp in range(2):
            s_bufs[slot][mp][...] = jnp.dot(kc, qts[mp], preferred_element_type=F32)

    def accumulate(c, slot, alphas):
        vt = vt_ref[:, pl.ds(pl.multiple_of(c * tk, tk), tk)]
        for mp in range(2):
            accs[mp][...] = alphas[mp] * accs[mp][...] + jnp.dot(vt, p_bufs[slot][mp][...],
                                                                 preferred_element_type=F32)

    def softmax(slot, stats):
        new, alphas = [], []
        for mp in range(2):
            m, l = stats[2 * mp:2 * mp + 2]
            s = s_bufs[slot][mp][...]
            m_new = jnp.maximum(m, jnp.max(s, axis=0, keepdims=True))
            alpha = jnp.exp2(m - m_new)
            p = jnp.exp2(s - m_new)
            p_bufs[slot][mp][...] = p.astype(BF16)
            new += [m_new, alpha * l + jnp.sum(p, axis=0, keepdims=True)]
            alphas.append(alpha)
        return tuple(new), tuple(alphas)

    for mp in range(2):
        accs[mp][...] = jnp.zeros_like(accs[mp])
        p_bufs[1][mp][...] = jnp.zeros_like(p_bufs[1][mp])
    scores(0, 0)
    stats = (jnp.full((1, tq), NEG_INF, F32), jnp.zeros((1, tq), F32)) * 2
    ones = (jnp.ones((1, tq), F32),) * 2
    assert n_chunks % 2 == 1 and k_ref.shape[0] == n_chunks * tk

    n_pairs = (n_chunks - 1) // 2
    unroll = math.gcd(n_pairs, DA_UNROLL) if n_pairs else 1

    def pairs(t, carry):
        stats, alphas1 = carry
        for u in range(unroll):
            c = 2 * (t * unroll + u)
            scores(c + 1, 1)
            accumulate(jnp.maximum(c - 1, 0), 1, alphas1)
            stats, alphas0 = softmax(0, stats)
            scores(c + 2, 0)
            accumulate(c, 0, alphas0)
            stats, alphas1 = softmax(1, stats)
        return stats, alphas1

    stats, alphas1 = lax.fori_loop(0, n_pairs // unroll, pairs, (stats, ones))
    last = n_chunks - 1
    accumulate(jnp.maximum(last - 1, 0), 1, alphas1)
    stats, alphas0 = softmax(0, stats)
    accumulate(last, 0, alphas0)

    lp = lam_ref[...]
    lam = (jnp.exp(jnp.sum(lp[0:1] * lp[1:2], axis=-1, keepdims=True))
           - jnp.exp(jnp.sum(lp[2:3] * lp[3:4], axis=-1, keepdims=True)) + lam_init)
    ot = acc0[...] / stats[1] - lam * (acc1[...] / stats[3])
    o = _rms(ot.T) * subln_ref[...] * (1.0 - lam_init)
    o_ref[...] = o.astype(o_ref.dtype)


def da_attention(B, S, Lc, q_src, k_all, vt_all, lam_p, subln, lam_init, want_ctx):
    T = q_src.shape[0]
    width = DA_HEADS * 2 * HEAD_DIM
    n_keys = Lc + S
    small = (jnp.asarray(lam_p), subln.reshape(1, 2 * HEAD_DIM))
    small_specs = [pl.BlockSpec((4, HEAD_DIM), lambda b, h, i: (0, 0)),
                   pl.BlockSpec((1, 2 * HEAD_DIM), lambda b, h, i: (0, 0))]

    def call(tq, tk, n_rows_k, k_blocks_per_batch, k_block0, nq, q_block0, prev):
        n_chunks = n_rows_k // tk
        kern = functools.partial(_da_attn_kernel, lam_init=lam_init, tk=tk, n_chunks=n_chunks)
        q_idx = lambda b, h, i: (q_block0 + b * nq + i, h)
        in_specs = small_specs + [pl.BlockSpec((tq, LANES), q_idx),
                                  pl.BlockSpec((n_rows_k, LANES),
                                               lambda b, h, i: (b * k_blocks_per_batch + k_block0, h)),
                                  pl.BlockSpec((LANES, n_rows_k),
                                               lambda b, h, i: (h, b * k_blocks_per_batch + k_block0))]
        args = small + (q_src, k_all, vt_all)
        aliases = {}
        body = kern
        if prev is not None:
            in_specs.append(pl.BlockSpec(memory_space=pl.ANY))
            args += (prev,)
            aliases = {len(args) - 1: 0}
            body = lambda *refs: kern(*refs[:5], *refs[6:])
        return pl.pallas_call(
            body, grid=(B, DA_HEADS, nq), in_specs=in_specs,
            out_specs=pl.BlockSpec((tq, LANES), q_idx),
            out_shape=jax.ShapeDtypeStruct((T, width), BF16),
            scratch_shapes=([pltpu.VMEM((LANES, tq), F32)] * 2 + [pltpu.VMEM((tk, tq), F32)] * 4
                            + [pltpu.VMEM((tk, tq), BF16)] * 4),
            input_output_aliases=aliases,
            compiler_params=_cparams(("parallel", "parallel", "arbitrary")),
            name="da_attention" if prev is None else "da_attention_ctx",
        )(*args)

    tq = min(DA_TQ, S)
    tk = next(t for t in DA_TK if n_keys % t == 0 and (n_keys // t) % 2 == 1)
    o = call(tq, tk, n_keys, 1, 0, S // tq, 0, None)
    if want_ctx:
        assert S % Lc == 0
        o = call(Lc, Lc, Lc, n_keys // Lc, S // Lc, 1, B * S // Lc, o)
    return o


def _sw_attn_kernel(sink_ref, q_ref, kc_ref, vc_ref, kp_ref, kx_ref, kn_ref, vp_ref, vx_ref, vn_ref,
                    o_ref, *, n_lat_q, nb):
    tq = q_ref.shape[0]
    Lc = kc_ref.shape[0]
    i = pl.program_id(1)
    is_lat = i < n_lat_q
    n = i % nb
    krel = lax.broadcasted_iota(jnp.int32, (3 * SW_BLOCK, tq), 0) - SW_BLOCK
    qrow = lax.broadcasted_iota(jnp.int32, (3 * SW_BLOCK, tq), 1)
    kabs = krel + n * SW_BLOCK
    valid = (jnp.abs(qrow - krel) <= SW_WINDOW) & (kabs >= 0) & (kabs < nb * SW_BLOCK) & is_lat
    valid = jnp.concatenate([valid] * SW_GROUP, axis=1)
    row = lax.broadcasted_iota(jnp.int32, (LANES, SW_GROUP * tq), 0)
    lo = row < HEAD_DIM
    for j in range(SW_KV_HEADS // 2):
        ksl = slice(j * LANES, (j + 1) * LANES)
        keys = jnp.concatenate([kc_ref[:, ksl], kp_ref[:, ksl], kx_ref[:, ksl], kn_ref[:, ksl]], axis=0)
        vals = jnp.concatenate([vc_ref[:, ksl], vp_ref[:, ksl], vx_ref[:, ksl], vn_ref[:, ksl]], axis=0)
        vals_t = vals.astype(F32).T.astype(BF16)
        qt = jnp.concatenate(
            [q_ref[:, (j * SW_GROUP + g) * LANES:(j * SW_GROUP + g + 1) * LANES].astype(F32).T
             for g in range(SW_GROUP)], axis=1)
        halves = []
        for e in range(2):
            sink = sink_ref[2 * j + e:2 * j + e + 1, :]
            qz = jnp.where(lo if e == 0 else ~lo, qt, 0.0).astype(BF16)
            s = jnp.dot(keys, qz, preferred_element_type=F32)
            s_ctx = s[:Lc]
            s_loc = jnp.where(valid, s[Lc:], NEG_INF)
            m = jnp.maximum(jnp.maximum(jnp.max(s_ctx, axis=0, keepdims=True),
                                        jnp.max(s_loc, axis=0, keepdims=True)), sink)
            p_ctx = jnp.exp(s_ctx - m)
            p_loc = jnp.exp(s_loc - m)
            denom = (jnp.sum(p_ctx, axis=0, keepdims=True) + jnp.sum(p_loc, axis=0, keepdims=True)
                     + jnp.exp(sink - m))
            p = jnp.concatenate([p_ctx, p_loc], axis=0).astype(BF16)
            halves.append(jnp.dot(vals_t, p, preferred_element_type=F32) / denom)
        ot = jnp.where(lo, halves[0], halves[1])
        for g in range(SW_GROUP):
            slab = j * SW_GROUP + g
            o_ref[:, slab * LANES:(slab + 1) * LANES] = ot[:, g * tq:(g + 1) * tq].T.astype(o_ref.dtype)


def sw_attention(B, S, Lc, qkv, sink, want_ctx):
    T = qkv.shape[0]
    qw = SW_HEADS * HEAD_DIM
    kvw = SW_KV_HEADS * HEAD_DIM
    tq = SW_BLOCK
    nb = S // SW_BLOCK
    nqc = Lc // tq
    nq = nb + (nqc if want_ctx else 0)
    kcol, vcol = qw // kvw, qw // kvw + 1

    def q_idx(b, i):
        return (jnp.where(i < nb, b * nb + i, B * nb + b * nqc + (i - nb)), 0)

    def band(col, off):
        def idx(b, i):
            n = jnp.clip(jnp.where(i < nb, i, 0) + off, 0, nb - 1)
            return (b * nb + n, col)
        return pl.BlockSpec((SW_BLOCK, kvw), idx)

    sink_lanes = jnp.repeat(sink.astype(F32).reshape(SW_KV_HEADS, SW_GROUP), tq, axis=1)
    kern = functools.partial(_sw_attn_kernel, n_lat_q=nb, nb=nb)
    return pl.pallas_call(
        kern, grid=(B, nq),
        in_specs=[pl.BlockSpec((SW_KV_HEADS, SW_GROUP * tq), lambda b, i: (0, 0)),
                  pl.BlockSpec((tq, qw), q_idx),
                  pl.BlockSpec((Lc, kvw), lambda b, i: (B * S // Lc + b, kcol)),
                  pl.BlockSpec((Lc, kvw), lambda b, i: (B * S // Lc + b, vcol)),
                  band(kcol, -1), band(kcol, 0), band(kcol, 1),
                  band(vcol, -1), band(vcol, 0), band(vcol, 1)],
        out_specs=pl.BlockSpec((tq, qw), q_idx),
        out_shape=jax.ShapeDtypeStruct((T, qw), BF16),
        compiler_params=_cparams(("parallel", "arbitrary")), name="sw_attention",
    )(sink_lanes, qkv, qkv, qkv, qkv, qkv, qkv, qkv, qkv, qkv)


def _sw_head_perm():
    cols = []
    for j in range(SW_KV_HEADS // 2):
        for g in range(SW_GROUP):
            for e in range(2):
                head = (2 * j + e) * SW_GROUP + g
                cols += list(range(head * HEAD_DIM, (head + 1) * HEAD_DIM))
    return np.asarray(cols, dtype=np.int32)


def _ssd_conv_kernel(prev_ref, cur_ref, next_ref, w_ref, b_ref, o_ref, ext_ref, *, seg_lat, seg_ctx, n_lat_blk):
    tc = cur_ref.shape[0]
    i = pl.program_id(0)
    row0 = i * tc
    is_lat = i < n_lat_blk
    seg = jnp.where(is_lat, seg_lat, seg_ctx)
    rel = jnp.where(is_lat, row0, row0 - n_lat_blk * tc) % seg
    keep_prev = (rel != 0).astype(F32)
    keep_next = (rel + tc != seg).astype(F32)
    ext_ref[0:CONV_HALO, :] = prev_ref[...].astype(F32) * keep_prev
    ext_ref[CONV_HALO:CONV_HALO + tc, :] = cur_ref[...].astype(F32)
    ext_ref[CONV_HALO + tc:, :] = next_ref[...].astype(F32) * keep_next
    acc = jnp.zeros(o_ref.shape, F32) + b_ref[...]
    for k in range(SSD_CONV):
        off = CONV_HALO - SSD_CONV // 2 + k
        acc = acc + ext_ref[off:off + tc, :] * w_ref[k:k + 1, :]
    o_ref[...] = _silu(acc).astype(o_ref.dtype)


def ssd_conv(B, S, Lc, xbc, conv_w, conv_b):
    T, C = xbc.shape
    tc = min(256, Lc)
    cwid = next(w for w in (1536, 1024, 512) if C % w == 0)
    nblk = T // tc
    hb = tc // CONV_HALO
    nh = T // CONV_HALO
    kern = functools.partial(_ssd_conv_kernel, seg_lat=S, seg_ctx=Lc, n_lat_blk=B * S // tc)
    return pl.pallas_call(
        kern, grid=(nblk, C // cwid),
        in_specs=[pl.BlockSpec((CONV_HALO, cwid), lambda i, j: (jnp.maximum(i * hb - 1, 0), j)),
                  pl.BlockSpec((tc, cwid), lambda i, j: (i, j)),
                  pl.BlockSpec((CONV_HALO, cwid), lambda i, j: (jnp.minimum((i + 1) * hb, nh - 1), j)),
                  pl.BlockSpec((SSD_CONV, cwid), lambda i, j: (0, j)),
                  pl.BlockSpec((1, cwid), lambda i, j: (0, j))],
        out_specs=pl.BlockSpec((tc, cwid), lambda i, j: (i, j)),
        out_shape=jax.ShapeDtypeStruct((T, C), BF16),
        scratch_shapes=[pltpu.VMEM((tc + 2 * CONV_HALO, cwid), F32)],
        compiler_params=_cparams(("parallel", "parallel")), name="ssd_conv",
    )(xbc, xbc, xbc, conv_w, conv_b.reshape(1, C))


def _split3(t):
    hi = t.astype(BF16)
    r1 = t - hi.astype(F32)
    mid = r1.astype(BF16)
    lo = (r1 - mid.astype(F32)).astype(BF16)
    return hi, mid, lo


def _dot01(t, ones_mat):
    out = None
    for part in _split3(t):
        term = jnp.dot(part, ones_mat, preferred_element_type=F32)
        out = term if out is None else out + term
    return out


def _dot_left01(ones_mat, t):
    out = None
    for part in _split3(t):
        term = jnp.dot(ones_mat, part, preferred_element_type=F32)
        out = term if out is None else out + term
    return out


def _ssd_scan_kernel(xf_ref, xb_ref, dtf_ref, dtb_ref, aneg_ref, bias_ref, ef_ref, eb_ref,
                     yf_ref, yb_ref, sf_ref, sb_ref, *, inner, n_heads):
    @pl.when(pl.program_id(1) == 0)
    def _():
        sf_ref[...] = jnp.zeros_like(sf_ref)
        sb_ref[...] = jnp.zeros_like(sb_ref)

    for reverse, x_ref, dt_ref, e_ref, y_ref, s_ref in (
            (False, xf_ref, dtf_ref, ef_ref, yf_ref, sf_ref),
            (True, xb_ref, dtb_ref, eb_ref, yb_ref, sb_ref)):
        _ssd_chunk(x_ref, dt_ref, aneg_ref, bias_ref, e_ref, y_ref, s_ref,
                   reverse=reverse, inner=inner, n_heads=n_heads)


def _ssd_chunk(x_ref, dt_ref, aneg_ref, bias_ref, e_ref, y_ref, s_ref, *, reverse, inner, n_heads):
    L = x_ref.shape[0]
    gn = SSD_GROUPS * SSD_STATE
    hpg = n_heads // SSD_GROUPS
    gw = hpg * SSD_HEAD_DIM
    head0 = n_heads if reverse else 0

    z = dt_ref[...] + bias_ref[...]
    dt = jnp.maximum(z, 0.0) + jnp.log(1.0 + jnp.exp(-jnp.abs(z)))
    a = dt * (-jnp.exp(aneg_ref[...]))
    r = lax.broadcasted_iota(jnp.int32, (L, L), 0)
    c = lax.broadcasted_iota(jnp.int32, (L, L), 1)
    causal = (c >= r) if reverse else (c <= r)
    tri = jnp.where(causal, 1.0, 0.0).astype(BF16)
    acs = _dot_left01(tri, a)
    a_tot = jnp.sum(a, axis=0, keepdims=True)
    dte = jnp.exp(a_tot - acs)
    eacs = jnp.exp(acs)
    acs_t = acs.T

    dt_t = dt.T

    pk = 2 * SUBLANES
    eacs_hi, eacs_mid, _ = _split3(eacs)
    cdec_parts = [jnp.broadcast_to(part, (pk, LANES)) for part in _split3(jnp.exp(a_tot))]
    stack = jnp.concatenate([(dt * dte).astype(BF16), eacs_hi, eacs_mid] + cdec_parts, axis=0)
    ex = jnp.dot(stack, e_ref[...], preferred_element_type=F32)
    dtdte_e = ex[0:L]
    eacs_e = ex[L:2 * L] + ex[2 * L:3 * L]
    cdec_e = ex[3 * L:3 * L + 1] + ex[3 * L + pk:3 * L + pk + 1] + ex[3 * L + 2 * pk:3 * L + 2 * pk + 1]

    xw = (x_ref[:, :inner].astype(F32) * dtdte_e).astype(BF16)
    lane = lax.broadcasted_iota(jnp.int32, (L, LANES), 1)
    lo = lane < SSD_HEAD_DIM
    nt = (((1,), (1,)), ((), ()))
    tn = (((0,), (0,)), ((), ()))
    for g in range(SSD_GROUPS):
        Bg = x_ref[:, inner + g * SSD_STATE:inner + (g + 1) * SSD_STATE]
        Cg = x_ref[:, inner + gn + g * SSD_STATE:inner + gn + (g + 1) * SSD_STATE]
        gs = slice(g * gw, (g + 1) * gw)
        cb = lax.dot_general(Cg, Bg, nt, preferred_element_type=F32)
        s_prev = s_ref[:, gs]
        y_off = jnp.dot(Cg, s_prev.astype(BF16), preferred_element_type=F32) * eacs_e[:, gs]
        for k in range(hpg // 2):
            sl = slice(g * gw + k * LANES, g * gw + (k + 1) * LANES)
            halves = []
            for eh in range(2):
                col = head0 + g * hpg + 2 * k + eh
                seg = acs[:, col:col + 1] - acs_t[col:col + 1, :]
                w = cb * jnp.exp(jnp.where(causal, seg, NEG_INF)) * dt_t[col:col + 1, :]
                halves.append(jnp.dot(w.astype(BF16), x_ref[:, sl], preferred_element_type=F32))
            y = jnp.where(lo, halves[0], halves[1]) + y_off[:, k * LANES:(k + 1) * LANES]
            y_ref[:, sl] = y.astype(y_ref.dtype)
        s_ref[:, gs] = s_prev * cdec_e[:, gs] + lax.dot_general(Bg, xw[:, gs], tn, preferred_element_type=F32)


def ssd_scan(B, S, Lc, xbc, dt_raw, a_log_pad, dt_bias_pad, inner, n_heads):
    T, C = xbc.shape
    L = SSD_CHUNK
    nl, nc = S // L, Lc // L
    nsteps = nl + nc

    def fwd_idx(b, i):
        return (jnp.where(i < nc, B * nl + b * nc + i, b * nl + (i - nc)), 0)

    def bwd_idx(b, i):
        return (jnp.where(i < nc, B * nl + b * nc + (nc - 1 - i), b * nl + (nl - 1 - (i - nc))), 0)

    hd = np.arange(inner) // SSD_HEAD_DIM
    ef = (np.arange(LANES)[:, None] == hd[None, :]).astype(np.float32)
    eb = (np.arange(LANES)[:, None] == (hd[None, :] + n_heads)).astype(np.float32)
    kern = functools.partial(_ssd_scan_kernel, inner=inner, n_heads=n_heads)
    const = lambda shape: pl.BlockSpec(shape, lambda b, i: (0, 0))
    return pl.pallas_call(
        kern, grid=(B, nsteps),
        in_specs=[pl.BlockSpec((L, C), fwd_idx), pl.BlockSpec((L, C), bwd_idx),
                  pl.BlockSpec((L, LANES), fwd_idx), pl.BlockSpec((L, LANES), bwd_idx),
                  const((1, LANES)), const((1, LANES)), const((LANES, inner)), const((LANES, inner))],
        out_specs=[pl.BlockSpec((L, inner), fwd_idx), pl.BlockSpec((L, inner), bwd_idx)],
        out_shape=[jax.ShapeDtypeStruct((T, inner), BF16)] * 2,
        scratch_shapes=[pltpu.VMEM((SSD_STATE, inner), F32)] * 2,
        compiler_params=_cparams(("parallel", "arbitrary")), name="ssd_scan",
    )(xbc, xbc, dt_raw, dt_raw, a_log_pad, dt_bias_pad, jnp.asarray(ef, BF16), jnp.asarray(eb, BF16))


def _rope_tables(S, tm):
    n_rows = S // GRID_W
    rows = jnp.repeat(jnp.arange(n_rows), GRID_W)
    cols = jnp.tile(jnp.arange(GRID_W), n_rows)
    n_freq = HEAD_DIM // 4
    freqs = ROPE_THETA ** (-jnp.arange(n_freq, dtype=F32) / n_freq)
    ang = jnp.concatenate([rows[:, None].astype(F32) * freqs, cols[:, None].astype(F32) * freqs], axis=-1)
    cos, sin = jnp.cos(ang), jnp.sin(ang)
    cos_t = jnp.concatenate([cos, cos, cos, cos], axis=-1)
    sin_t = jnp.concatenate([-sin, sin, -sin, sin], axis=-1)
    cos_t = jnp.concatenate([cos_t, jnp.ones((tm, LANES), F32)], axis=0)
    sin_t = jnp.concatenate([sin_t, jnp.zeros((tm, LANES), F32)], axis=0)
    return cos_t, sin_t


def kernel(x, c, ctx, c_ctx, ada_w, ada_b, norm_g, ffn_w_in, ffn_w_out, da_w_qkv, da_w_o, da_lambda, da_subln,
           sw_w_qkv, sw_w_o, sw_sink, ssd_w_in, ssd_conv_w, ssd_conv_b, ssd_a_log, ssd_dt_bias, ssd_d_skip,
           ssd_norm, ssd_w_out):
    B, S, D = x.shape
    Lc = ctx.shape[1]
    depth = ada_w.shape[0]
    tm = min(512, math.gcd(S, B * Lc))
    lay = Layout(B, S, Lc, tm)

    n_cond = -(-(B + 1) // SUBLANES) * SUBLANES
    cond = jnp.concatenate([c, c_ctx[None, :], jnp.zeros((n_cond - B - 1, D), F32)], axis=0)
    mods = ada_modulation(cond, ada_w, ada_b)
    rope = _rope_tables(S, tm)
    lay_kv = Layout(B, S, Lc, min(256, math.gcd(S, Lc)))
    rope_kv = _rope_tables(S, lay_kv.tm)

    h = jnp.concatenate([x.reshape(B * S, D), ctx.reshape(B * Lc, D)], axis=0)
    da_width = DA_HEADS * 2 * HEAD_DIM
    sw_qw = SW_HEADS * HEAD_DIM
    sw_kvw = SW_KV_HEADS * HEAD_DIM
    perm = _sw_head_perm()

    for i in range(depth):
        want_ctx = i < depth - 1
        j = i // N_MIXERS
        kind = i % N_MIXERS
        mod = mods[i]
        g = norm_g[i]
        nblk = None if want_ctx else lay.nlat
        if kind == 0:
            lam_init = 0.8 - 0.6 * math.exp(-0.3 * i)
            w = da_w_qkv[j]
            q_scale = HEAD_DIM ** -0.5 * math.log2(math.e)
            w = jnp.concatenate([w[:, :da_width] * q_scale, w[:, da_width:]], axis=1).astype(BF16)
            n_keys = S + Lc
            kv_rows = (B * n_keys, lay_kv.kv_block, False)
            q, k_all, vt_all = in_proj(
                lay_kv, h, mod, g[0], w, [(0, da_width), (da_width, da_width), (2 * da_width, da_width)],
                [BF16] * 3, rope=rope_kv, rope_cols=2 * da_width,
                out_rows=[None, kv_rows, (B * n_keys, lay_kv.kv_block, True)], name="da_in_proj")
            o = da_attention(B, S, Lc, q, k_all, vt_all, da_lambda[j], da_subln[j], lam_init, want_ctx)
            mixer = (o, da_w_o[j].astype(BF16))
        elif kind == 1:
            w = sw_w_qkv[j]
            w = jnp.concatenate([w[:, :sw_qw][:, perm] * HEAD_DIM ** -0.5, w[:, sw_qw:]], axis=1).astype(BF16)
            (qkv,) = in_proj(lay, h, mod, g[0], w, [(0, sw_qw + 2 * sw_kvw)], [BF16], rope=rope,
                             rope_cols=sw_qw + sw_kvw, cw=256, name="sw_in_proj")
            o = sw_attention(B, S, Lc, qkv, sw_sink[j], want_ctx)
            mixer = (o, sw_w_o[j][perm, :].astype(BF16))
        else:
            n_heads = ssd_a_log.shape[2]
            inner = n_heads * SSD_HEAD_DIM
            conv_ch = inner + 2 * SSD_GROUPS * SSD_STATE
            w = ssd_w_in[j]
            w = jnp.pad(w, ((0, 0), (0, LANES - 2 * n_heads))).astype(BF16)
            z, xbc, dt_raw = in_proj(lay, h, mod, g[0], w,
                                     [(0, inner), (inner, conv_ch), (inner + conv_ch, LANES)],
                                     [BF16, BF16, F32], name="ssd_in_proj")
            xbc = ssd_conv(B, S, Lc, xbc, ssd_conv_w[j], ssd_conv_b[j])
            pad = lambda t: jnp.pad(t.reshape(1, 2 * n_heads), ((0, 0), (0, LANES - 2 * n_heads)))
            yf, yb = ssd_scan(B, S, Lc, xbc, dt_raw, pad(ssd_a_log[j]), pad(ssd_dt_bias[j]), inner, n_heads)
            dskip = jnp.repeat(ssd_d_skip[j], SSD_HEAD_DIM).reshape(1, inner)
            h = out_proj(lay, [yf, yb, xbc, z], ssd_w_out[j].astype(BF16), h, mod, g[1],
                         ssd_extra=(dskip, ssd_norm[j].reshape(1, inner)), nblk=nblk, name="ssd_out_proj")
            mixer = None
        h = ffn(lay, h, mod, g, ffn_w_in[i].astype(BF16), ffn_w_out[i].astype(BF16), mixer=mixer, nblk=nblk)
    return h[:B * S].reshape(B, S, D)
```

```python
import functools
import math

import numpy as np
import jax
import jax.numpy as jnp
from jax import lax
from jax.experimental import pallas as pl
from jax.experimental.pallas import tpu as pltpu

F32 = jnp.float32
BF16 = jnp.bfloat16

NORM_EPS = 1e-6
ROPE_THETA = 10000.0
NEG_INF = -1e30
GRID_W = 64
N_MOD = 6
N_MIXERS = 3

LANES = 128
SUBLANES = 8
VMEM_LIMIT = 56 * 1024 * 1024

HEAD_DIM = 64
DA_HEADS = 8
SW_HEADS = 16
SW_KV_HEADS = 4
SW_GROUP = SW_HEADS // SW_KV_HEADS
SW_WINDOW = 128
SW_BLOCK = 128
SSD_HEAD_DIM = 64
SSD_GROUPS = 4
SSD_STATE = 128
SSD_CONV = 5
SSD_CHUNK = 128
DA_TQ = 256
DA_TK = (256,)
DA_SUM_ROWS = 16
DA_TILES = 1
DA_UNROLL = 16
CONV_HALO = 8


def _cparams(sem):
    return pltpu.CompilerParams(dimension_semantics=sem, vmem_limit_bytes=VMEM_LIMIT)


def _resident(shape):
    nd = len(shape)
    return pl.BlockSpec(shape, lambda *_: (0,) * nd, pipeline_mode=pl.Buffered(1))


def _rms(t):
    return t * lax.rsqrt(jnp.mean(t * t, axis=-1, keepdims=True) + NORM_EPS)


def _silu(t):
    return t * jax.nn.sigmoid(t)


def _ada_kernel(c_ref, w_ref, b_ref, o_ref):
    act = _silu(c_ref[...])
    o_ref[...] = jnp.dot(act, w_ref[...], precision=lax.Precision.HIGHEST,
                         preferred_element_type=F32) + b_ref[...]


def ada_modulation(cond, ada_w, ada_b):
    L, D, N = ada_w.shape
    R = cond.shape[0]
    tn = 1536
    out = pl.pallas_call(
        _ada_kernel,
        grid=(L, N // tn),
        in_specs=[pl.BlockSpec((R, D), lambda l, j: (0, 0)),
                  pl.BlockSpec((None, D, tn), lambda l, j: (l, 0, j)),
                  pl.BlockSpec((None, 1, tn), lambda l, j: (l, 0, j))],
        out_specs=pl.BlockSpec((None, R, tn), lambda l, j: (l, 0, j)),
        out_shape=jax.ShapeDtypeStruct((L, R, N), F32),
        compiler_params=_cparams(("arbitrary", "arbitrary")),
        name="ada_modulation",
    )(cond, ada_w, ada_b.reshape(L, 1, N))
    return out.reshape(L, R, N_MOD, D)


class Layout:
    def __init__(self, B, S, Lc, tm):
        assert S % tm == 0 and (B * Lc) % tm == 0
        self.B, self.S, self.Lc, self.tm = B, S, Lc, tm
        self.T = B * S + B * Lc
        self.npb = S // tm
        self.nlat = B * self.npb
        self.nblk = self.T // tm

    def mod_idx(self, i):
        return jnp.minimum(i // self.npb, self.B)

    def pos_idx(self, i):
        return jnp.where(i < self.nlat, i % self.npb, self.npb)

    def kv_block(self, i):
        per_batch = (self.S + self.Lc) // self.tm
        ncb = self.Lc // self.tm
        ci = i - self.nlat
        return jnp.where(i < self.nlat, (i // self.npb) * per_batch + i % self.npb,
                         (ci // ncb) * per_batch + self.npb + ci % ncb)


def _rope_slab(y, cos, sin_signed, lane):
    fwd = pltpu.roll(y, LANES - HEAD_DIM // 2, 1)
    bwd = pltpu.roll(y, HEAD_DIM // 2, 1)
    rot = jnp.where(lane % HEAD_DIM < HEAD_DIM // 2, fwd, bwd)
    return y * cos + rot * sin_signed


def _in_proj_kernel(*refs, segs, transposed, rope_cols, shift_row, scale_row, cw):
    h_ref, mod_ref, g_ref, w_ref = refs[:4]
    k = 4
    if rope_cols:
        cos_ref, sin_ref = refs[4:6]
        k = 6
    out_refs = refs[k:]
    h = h_ref[...]
    u = _rms(h) * g_ref[...]
    u = u * (1.0 + mod_ref[scale_row:scale_row + 1, :]) + mod_ref[shift_row:shift_row + 1, :]
    ub = u.astype(BF16)
    if rope_cols:
        cos = cos_ref[...]
        sin = sin_ref[...]
        lane = lax.broadcasted_iota(jnp.int32, cos.shape, 1)
    for o_ref, (col0, width), tr in zip(out_refs, segs, transposed):
        for c0 in range(0, width, cw):
            w_c = min(cw, width - c0)
            y = jnp.dot(ub, w_ref[:, col0 + c0:col0 + c0 + w_c], preferred_element_type=F32)
            if col0 + c0 < rope_cols:
                assert col0 + c0 + w_c <= rope_cols and w_c % LANES == 0
                y = jnp.concatenate(
                    [_rope_slab(y[:, s:s + LANES], cos, sin, lane) for s in range(0, w_c, LANES)], axis=1)
            if tr:
                o_ref[c0:c0 + w_c, :] = y.T.astype(o_ref.dtype)
            else:
                o_ref[:, c0:c0 + w_c] = y.astype(o_ref.dtype)


def in_proj(lay, h, mod, g, w, segs, out_dtypes, *, rope=None, rope_cols=0, shift_row=0, scale_row=1,
            cw=512, out_rows=None, name="in_proj"):
    tm, D = lay.tm, h.shape[1]
    N = w.shape[1]
    out_rows = out_rows or [None] * len(segs)
    out_rows = [r or (lay.T, lambda i: i, False) for r in out_rows]
    in_specs = [pl.BlockSpec((tm, D), lambda i: (i, 0)),
                pl.BlockSpec((None, N_MOD, D), lambda i: (lay.mod_idx(i), 0, 0)),
                pl.BlockSpec((1, D), lambda i: (0, 0)),
                _resident((D, N))]
    args = [h, mod, g.reshape(1, D), w]
    if rope_cols:
        cos, sin = rope
        in_specs += [pl.BlockSpec((tm, LANES), lambda i: (lay.pos_idx(i), 0))] * 2
        args += [cos, sin]
    out_specs, out_shape = [], []
    for (_, width), dt, (n_rows, blk, tr) in zip(segs, out_dtypes, out_rows):
        if tr:
            out_specs.append(pl.BlockSpec((width, tm), lambda i, blk=blk: (0, blk(i))))
            out_shape.append(jax.ShapeDtypeStruct((width, n_rows), dt))
        else:
            out_specs.append(pl.BlockSpec((tm, width), lambda i, blk=blk: (blk(i), 0)))
            out_shape.append(jax.ShapeDtypeStruct((n_rows, width), dt))
    kern = functools.partial(_in_proj_kernel, segs=tuple(segs), transposed=tuple(r[2] for r in out_rows),
                             rope_cols=rope_cols, shift_row=shift_row, scale_row=scale_row, cw=cw)
    return pl.pallas_call(
        kern, grid=(lay.nblk,), in_specs=in_specs, out_specs=out_specs, out_shape=out_shape,
        compiler_params=_cparams(("parallel",)), name=name,
    )(*args)


def _out_proj_kernel(*refs, ssd, gate_row):
    if ssd:
        yf_ref, yb_ref, xbc_ref, z_ref, dskip_ref, gn_ref, w_ref, h_ref, mod_ref, g_ref, o_ref = refs
        inner = yf_ref.shape[1]
        gw = inner // SSD_GROUPS
        y = (yf_ref[...].astype(F32) + yb_ref[...].astype(F32)
             + xbc_ref[:, :inner].astype(F32) * dskip_ref[...])
        y = y * _silu(z_ref[...].astype(F32))
        y = jnp.concatenate([_rms(y[:, k * gw:(k + 1) * gw]) for k in range(SSD_GROUPS)], axis=1)
        a = (y * gn_ref[...]).astype(BF16)
    else:
        a_ref, w_ref, h_ref, mod_ref, g_ref, o_ref = refs
        a = a_ref[...]
    y = jnp.dot(a, w_ref[...], preferred_element_type=F32)
    o_ref[...] = h_ref[...] + mod_ref[gate_row:gate_row + 1, :] * (_rms(y) * g_ref[...])


def out_proj(lay, acts, w, h, mod, g, *, ssd_extra=None, gate_row=2, nblk=None, name="out_proj"):
    tm, D = lay.tm, h.shape[1]
    K = w.shape[0]
    nblk = lay.nblk if nblk is None else nblk
    row = lambda width: pl.BlockSpec((tm, width), lambda i: (i, 0))
    in_specs = [row(a.shape[1]) for a in acts]
    args = list(acts)
    if ssd_extra is not None:
        dskip, gn = ssd_extra
        in_specs += [pl.BlockSpec((1, K), lambda i: (0, 0))] * 2
        args += [dskip, gn]
    in_specs += [_resident((K, D)), row(D),
                 pl.BlockSpec((None, N_MOD, D), lambda i: (lay.mod_idx(i), 0, 0)),
                 pl.BlockSpec((1, D), lambda i: (0, 0))]
    args += [w, h, mod, g.reshape(1, D)]
    kern = functools.partial(_out_proj_kernel, ssd=ssd_extra is not None, gate_row=gate_row)
    return pl.pallas_call(
        kern, grid=(nblk,), in_specs=in_specs, out_specs=row(D),
        out_shape=jax.ShapeDtypeStruct((nblk * tm, D), F32),
        compiler_params=_cparams(("parallel",)), name=name,
    )(*args)


def _ffn_kernel(*refs, hidden, cw, with_mixer):
    if with_mixer:
        a_ref, wo_ref, h_ref, mod_ref, g_ref, w_in_ref, w_out_ref, o_ref, act_ref = refs
        y = jnp.dot(a_ref[...], wo_ref[...], preferred_element_type=F32)
        h = h_ref[...] + mod_ref[2:3, :] * (_rms(y) * g_ref[1:2, :])
    else:
        h_ref, mod_ref, g_ref, w_in_ref, w_out_ref, o_ref, act_ref = refs
        h = h_ref[...]
    u = _rms(h) * g_ref[2:3, :]
    ub = (u * (1.0 + mod_ref[4:5, :]) + mod_ref[3:4, :]).astype(BF16)
    for c0 in range(0, hidden, cw):
        gate = jnp.dot(ub, w_in_ref[:, c0:c0 + cw], preferred_element_type=F32)
        up = jnp.dot(ub, w_in_ref[:, hidden + c0:hidden + c0 + cw], preferred_element_type=F32)
        act_ref[:, c0:c0 + cw] = (_silu(gate) * up).astype(BF16)
    y = jnp.dot(act_ref[...], w_out_ref[...], preferred_element_type=F32)
    o_ref[...] = h + mod_ref[5:6, :] * (_rms(y) * g_ref[3:4, :])


def ffn(lay, h, mod, g4, w_in, w_out, *, mixer=None, nblk=None, name="ffn"):
    tm, D = lay.tm, h.shape[1]
    hidden = w_out.shape[0]
    nblk = lay.nblk if nblk is None else nblk
    cw = 256
    assert hidden % cw == 0
    row = lambda width: pl.BlockSpec((tm, width), lambda i: (i, 0))
    in_specs, args = [], []
    if mixer is not None:
        a, w_o = mixer
        in_specs += [row(a.shape[1]), _resident(w_o.shape)]
        args += [a, w_o]
    in_specs += [row(D), pl.BlockSpec((None, N_MOD, D), lambda i: (lay.mod_idx(i), 0, 0)),
                 pl.BlockSpec((4, D), lambda i: (0, 0)),
                 _resident((D, 2 * hidden)), _resident((hidden, D))]
    args += [h, mod, g4, w_in, w_out]
    kern = functools.partial(_ffn_kernel, hidden=hidden, cw=cw, with_mixer=mixer is not None)
    return pl.pallas_call(
        kern, grid=(nblk,), in_specs=in_specs, out_specs=row(D),
        out_shape=jax.ShapeDtypeStruct((nblk * tm, D), F32),
        scratch_shapes=[pltpu.VMEM((tm, hidden), BF16)],
        compiler_params=_cparams(("parallel",)), name=name,
    )(*args)


def _da_attn_kernel(lam_ref, subln_ref, q_ref, k_ref, vt_ref, o_ref, *scratch, lam_init, tk, n_chunks, n_tiles):
    tq = q_ref.shape[0] // n_tiles
    ns = 2 * n_tiles
    accs = scratch[:ns]
    s_bufs = (scratch[ns:2 * ns], scratch[2 * ns:3 * ns])
    p_bufs = (scratch[3 * ns:4 * ns], scratch[4 * ns:5 * ns])
    qts = []
    for t in range(n_tiles):
        qt = q_ref[t * tq:(t + 1) * tq, :].astype(F32).T
        row = lax.broadcasted_iota(jnp.int32, qt.shape, 0)
        qts += [jnp.where(row < HEAD_DIM, qt, 0.0).astype(BF16), jnp.where(row >= HEAD_DIM, qt, 0.0).astype(BF16)]

    def scores(c, slot):
        kc = k_ref[pl.ds(pl.multiple_of(c * tk, tk), tk), :]
        for st in range(ns):
            s_bufs[slot][st][...] = jnp.dot(kc, qts[st], preferred_element_type=F32)

    def accumulate(c, slot, alphas):
        vt = jnp.concatenate([vt_ref[:, pl.ds(pl.multiple_of(c * tk, tk), tk)],
                              jnp.ones((DA_SUM_ROWS, tk), BF16)], axis=0)
        for st in range(ns):
            accs[st][...] = alphas[st] * accs[st][...] + jnp.dot(vt, p_bufs[slot][st][...],
                                                                 preferred_element_type=F32)

    def softmax(slot, maxes):
        new, alphas = [], []
        for st in range(ns):
            s = s_bufs[slot][st][...]
            m_new = jnp.maximum(maxes[st], jnp.max(s, axis=0, keepdims=True))
            p_bufs[slot][st][...] = jnp.exp2(s - m_new).astype(BF16)
            new.append(m_new)
            alphas.append(jnp.exp2(maxes[st] - m_new))
        return tuple(new), tuple(alphas)

    for st in range(ns):
        accs[st][...] = jnp.zeros_like(accs[st])
        p_bufs[1][st][...] = jnp.zeros_like(p_bufs[1][st])
    scores(0, 0)
    stats = (jnp.full((1, tq), NEG_INF, F32),) * ns
    ones = (jnp.ones((1, tq), F32),) * ns
    assert n_chunks % 2 == 1 and k_ref.shape[0] == n_chunks * tk

    n_pairs = (n_chunks - 1) // 2
    unroll = math.gcd(n_pairs, max(DA_UNROLL // n_tiles, 1)) if n_pairs else 1

    def pairs(t, carry):
        stats, alphas1 = carry
        for u in range(unroll):
            c = 2 * (t * unroll + u)
            scores(c + 1, 1)
            accumulate(jnp.maximum(c - 1, 0), 1, alphas1)
            stats, alphas0 = softmax(0, stats)
            scores(c + 2, 0)
            accumulate(c, 0, alphas0)
            stats, alphas1 = softmax(1, stats)
        return stats, alphas1

    stats, alphas1 = lax.fori_loop(0, n_pairs // unroll, pairs, (stats, ones))
    last = n_chunks - 1
    accumulate(jnp.maximum(last - 1, 0), 1, alphas1)
    stats, alphas0 = softmax(0, stats)
    accumulate(last, 0, alphas0)

    lp = lam_ref[...]
    lam = (jnp.exp(jnp.sum(lp[0:1] * lp[1:2], axis=-1, keepdims=True))
           - jnp.exp(jnp.sum(lp[2:3] * lp[3:4], axis=-1, keepdims=True)) + lam_init)
    for t in range(n_tiles):
        a0, a1 = accs[2 * t][...], accs[2 * t + 1][...]
        ot = a0[:LANES] / a0[LANES:LANES + 1] - lam * (a1[:LANES] / a1[LANES:LANES + 1])
        o = _rms(ot.T) * subln_ref[...] * (1.0 - lam_init)
        o_ref[t * tq:(t + 1) * tq, :] = o.astype(o_ref.dtype)


def da_attention(B, S, Lc, q_src, k_all, vt_all, lam_p, subln, lam_init, want_ctx):
    T = q_src.shape[0]
    width = DA_HEADS * 2 * HEAD_DIM
    n_keys = Lc + S
    small = (jnp.asarray(lam_p), subln.reshape(1, 2 * HEAD_DIM))
    small_specs = [pl.BlockSpec((4, HEAD_DIM), lambda b, h, i: (0, 0)),
                   pl.BlockSpec((1, 2 * HEAD_DIM), lambda b, h, i: (0, 0))]

    def call(tq, n_tiles, tk, n_rows_k, k_blocks_per_batch, k_block0, nq, q_block0, prev):
        n_chunks = n_rows_k // tk
        ns = 2 * n_tiles
        kern = functools.partial(_da_attn_kernel, lam_init=lam_init, tk=tk, n_chunks=n_chunks, n_tiles=n_tiles)
        q_idx = lambda b, h, i: (q_block0 + b * nq + i, h)
        in_specs = small_specs + [pl.BlockSpec((n_tiles * tq, LANES), q_idx),
                                  pl.BlockSpec((n_rows_k, LANES),
                                               lambda b, h, i: (b * k_blocks_per_batch + k_block0, h)),
                                  pl.BlockSpec((LANES, n_rows_k),
                                               lambda b, h, i: (h, b * k_blocks_per_batch + k_block0))]
        args = small + (q_src, k_all, vt_all)
        aliases = {}
        body = kern
        if prev is not None:
            in_specs.append(pl.BlockSpec(memory_space=pl.ANY))
            args += (prev,)
            aliases = {len(args) - 1: 0}
            body = lambda *refs: kern(*refs[:5], *refs[6:])
        return pl.pallas_call(
            body, grid=(B, DA_HEADS, nq), in_specs=in_specs,
            out_specs=pl.BlockSpec((n_tiles * tq, LANES), q_idx),
            out_shape=jax.ShapeDtypeStruct((T, width), BF16),
            scratch_shapes=([pltpu.VMEM((LANES + DA_SUM_ROWS, tq), F32)] * ns + [pltpu.VMEM((tk, tq), F32)] * (2 * ns)
                            + [pltpu.VMEM((tk, tq), BF16)] * (2 * ns)),
            input_output_aliases=aliases,
            compiler_params=_cparams(("parallel", "parallel", "arbitrary")),
            name="da_attention" if prev is None else "da_attention_ctx",
        )(*args)

    tq = min(DA_TQ, S)
    n_tiles = DA_TILES if S % (DA_TILES * tq) == 0 else 1
    tk = next(t for t in DA_TK if n_keys % t == 0 and (n_keys // t) % 2 == 1)
    o = call(tq, n_tiles, tk, n_keys, 1, 0, S // (n_tiles * tq), 0, None)
    if want_ctx:
        assert S % Lc == 0
        o = call(Lc, 1, Lc, Lc, n_keys // Lc, S // Lc, 1, B * S // Lc, o)
    return o


def _sw_attn_kernel(sink_ref, q_ref, kc_ref, vc_ref, kp_ref, kx_ref, kn_ref, vp_ref, vx_ref, vn_ref,
                    o_ref, *, n_lat_q, nb):
    tq = q_ref.shape[0]
    Lc = kc_ref.shape[0]
    i = pl.program_id(1)
    is_lat = i < n_lat_q
    n = i % nb
    krel = lax.broadcasted_iota(jnp.int32, (3 * SW_BLOCK, tq), 0) - SW_BLOCK
    qrow = lax.broadcasted_iota(jnp.int32, (3 * SW_BLOCK, tq), 1)
    kabs = krel + n * SW_BLOCK
    valid = (jnp.abs(qrow - krel) <= SW_WINDOW) & (kabs >= 0) & (kabs < nb * SW_BLOCK) & is_lat
    valid = jnp.concatenate([valid] * SW_GROUP, axis=1)
    row = lax.broadcasted_iota(jnp.int32, (LANES, SW_GROUP * tq), 0)
    lo = row < HEAD_DIM
    for j in range(SW_KV_HEADS // 2):
        ksl = slice(j * LANES, (j + 1) * LANES)
        keys = jnp.concatenate([kc_ref[:, ksl], kp_ref[:, ksl], kx_ref[:, ksl], kn_ref[:, ksl]], axis=0)
        vals = jnp.concatenate([vc_ref[:, ksl], vp_ref[:, ksl], vx_ref[:, ksl], vn_ref[:, ksl]], axis=0)
        vals_t = vals.astype(F32).T.astype(BF16)
        qt = jnp.concatenate(
            [q_ref[:, (j * SW_GROUP + g) * LANES:(j * SW_GROUP + g + 1) * LANES].astype(F32).T
             for g in range(SW_GROUP)], axis=1)
        halves = []
        for e in range(2):
            sink = sink_ref[2 * j + e:2 * j + e + 1, :]
            qz = jnp.where(lo if e == 0 else ~lo, qt, 0.0).astype(BF16)
            s = jnp.dot(keys, qz, preferred_element_type=F32)
            s_ctx = s[:Lc]
            s_loc = jnp.where(valid, s[Lc:], NEG_INF)
            m = jnp.maximum(jnp.maximum(jnp.max(s_ctx, axis=0, keepdims=True),
                                        jnp.max(s_loc, axis=0, keepdims=True)), sink)
            p_ctx = jnp.exp(s_ctx - m)
            p_loc = jnp.exp(s_loc - m)
            denom = (jnp.sum(p_ctx, axis=0, keepdims=True) + jnp.sum(p_loc, axis=0, keepdims=True)
                     + jnp.exp(sink - m))
            p = jnp.concatenate([p_ctx, p_loc], axis=0).astype(BF16)
            halves.append(jnp.dot(vals_t, p, preferred_element_type=F32) / denom)
        ot = jnp.where(lo, halves[0], halves[1])
        for g in range(SW_GROUP):
            slab = j * SW_GROUP + g
            o_ref[:, slab * LANES:(slab + 1) * LANES] = ot[:, g * tq:(g + 1) * tq].T.astype(o_ref.dtype)


def sw_attention(B, S, Lc, qkv, sink, want_ctx):
    T = qkv.shape[0]
    qw = SW_HEADS * HEAD_DIM
    kvw = SW_KV_HEADS * HEAD_DIM
    tq = SW_BLOCK
    nb = S // SW_BLOCK
    nqc = Lc // tq
    nq = nb + (nqc if want_ctx else 0)
    kcol, vcol = qw // kvw, qw // kvw + 1

    def q_idx(b, i):
        return (jnp.where(i < nb, b * nb + i, B * nb + b * nqc + (i - nb)), 0)

    def band(col, off):
        def idx(b, i):
            n = jnp.clip(jnp.where(i < nb, i, 0) + off, 0, nb - 1)
            return (b * nb + n, col)
        return pl.BlockSpec((SW_BLOCK, kvw), idx)

    sink_lanes = jnp.repeat(sink.astype(F32).reshape(SW_KV_HEADS, SW_GROUP), tq, axis=1)
    kern = functools.partial(_sw_attn_kernel, n_lat_q=nb, nb=nb)
    return pl.pallas_call(
        kern, grid=(B, nq),
        in_specs=[pl.BlockSpec((SW_KV_HEADS, SW_GROUP * tq), lambda b, i: (0, 0)),
                  pl.BlockSpec((tq, qw), q_idx),
                  pl.BlockSpec((Lc, kvw), lambda b, i: (B * S // Lc + b, kcol)),
                  pl.BlockSpec((Lc, kvw), lambda b, i: (B * S // Lc + b, vcol)),
                  band(kcol, -1), band(kcol, 0), band(kcol, 1),
                  band(vcol, -1), band(vcol, 0), band(vcol, 1)],
        out_specs=pl.BlockSpec((tq, qw), q_idx),
        out_shape=jax.ShapeDtypeStruct((T, qw), BF16),
        compiler_params=_cparams(("parallel", "arbitrary")), name="sw_attention",
    )(sink_lanes, qkv, qkv, qkv, qkv, qkv, qkv, qkv, qkv, qkv)


def _sw_head_perm():
    cols = []
    for j in range(SW_KV_HEADS // 2):
        for g in range(SW_GROUP):
            for e in range(2):
                head = (2 * j + e) * SW_GROUP + g
                cols += list(range(head * HEAD_DIM, (head + 1) * HEAD_DIM))
    return np.asarray(cols, dtype=np.int32)


def _ssd_conv_kernel(prev_ref, cur_ref, next_ref, w_ref, b_ref, o_ref, ext_ref, *, seg_lat, seg_ctx, n_lat_blk):
    tc = cur_ref.shape[0]
    i = pl.program_id(0)
    row0 = i * tc
    is_lat = i < n_lat_blk
    seg = jnp.where(is_lat, seg_lat, seg_ctx)
    rel = jnp.where(is_lat, row0, row0 - n_lat_blk * tc) % seg
    keep_prev = (rel != 0).astype(F32)
    keep_next = (rel + tc != seg).astype(F32)
    ext_ref[0:CONV_HALO, :] = prev_ref[...].astype(F32) * keep_prev
    ext_ref[CONV_HALO:CONV_HALO + tc, :] = cur_ref[...].astype(F32)
    ext_ref[CONV_HALO + tc:, :] = next_ref[...].astype(F32) * keep_next
    acc = jnp.zeros(o_ref.shape, F32) + b_ref[...]
    for k in range(SSD_CONV):
        off = CONV_HALO - SSD_CONV // 2 + k
        acc = acc + ext_ref[off:off + tc, :] * w_ref[k:k + 1, :]
    o_ref[...] = _silu(acc).astype(o_ref.dtype)


def ssd_conv(B, S, Lc, xbc, conv_w, conv_b):
    T, C = xbc.shape
    tc = min(256, Lc)
    cwid = next(w for w in (1536, 1024, 512) if C % w == 0)
    nblk = T // tc
    hb = tc // CONV_HALO
    nh = T // CONV_HALO
    kern = functools.partial(_ssd_conv_kernel, seg_lat=S, seg_ctx=Lc, n_lat_blk=B * S // tc)
    return pl.pallas_call(
        kern, grid=(nblk, C // cwid),
        in_specs=[pl.BlockSpec((CONV_HALO, cwid), lambda i, j: (jnp.maximum(i * hb - 1, 0), j)),
                  pl.BlockSpec((tc, cwid), lambda i, j: (i, j)),
                  pl.BlockSpec((CONV_HALO, cwid), lambda i, j: (jnp.minimum((i + 1) * hb, nh - 1), j)),
                  pl.BlockSpec((SSD_CONV, cwid), lambda i, j: (0, j)),
                  pl.BlockSpec((1, cwid), lambda i, j: (0, j))],
        out_specs=pl.BlockSpec((tc, cwid), lambda i, j: (i, j)),
        out_shape=jax.ShapeDtypeStruct((T, C), BF16),
        scratch_shapes=[pltpu.VMEM((tc + 2 * CONV_HALO, cwid), F32)],
        compiler_params=_cparams(("parallel", "parallel")), name="ssd_conv",
    )(xbc, xbc, xbc, conv_w, conv_b.reshape(1, C))


def _split3(t):
    hi = t.astype(BF16)
    r1 = t - hi.astype(F32)
    mid = r1.astype(BF16)
    lo = (r1 - mid.astype(F32)).astype(BF16)
    return hi, mid, lo


def _dot01(t, ones_mat):
    out = None
    for part in _split3(t):
        term = jnp.dot(part, ones_mat, preferred_element_type=F32)
        out = term if out is None else out + term
    return out


def _dot_left01(ones_mat, t):
    out = None
    for part in _split3(t):
        term = jnp.dot(ones_mat, part, preferred_element_type=F32)
        out = term if out is None else out + term
    return out


def _ssd_scan_kernel(xf_ref, xb_ref, dtf_ref, dtb_ref, aneg_ref, bias_ref, ef_ref, eb_ref,
                     yf_ref, yb_ref, sf_ref, sb_ref, *, inner, n_heads):
    @pl.when(pl.program_id(1) == 0)
    def _():
        sf_ref[...] = jnp.zeros_like(sf_ref)
        sb_ref[...] = jnp.zeros_like(sb_ref)

    for reverse, x_ref, dt_ref, e_ref, y_ref, s_ref in (
            (False, xf_ref, dtf_ref, ef_ref, yf_ref, sf_ref),
            (True, xb_ref, dtb_ref, eb_ref, yb_ref, sb_ref)):
        _ssd_chunk(x_ref, dt_ref, aneg_ref, bias_ref, e_ref, y_ref, s_ref,
                   reverse=reverse, inner=inner, n_heads=n_heads)


def _ssd_chunk(x_ref, dt_ref, aneg_ref, bias_ref, e_ref, y_ref, s_ref, *, reverse, inner, n_heads):
    L = x_ref.shape[0]
    gn = SSD_GROUPS * SSD_STATE
    hpg = n_heads // SSD_GROUPS
    gw = hpg * SSD_HEAD_DIM
    head0 = n_heads if reverse else 0

    z = dt_ref[...] + bias_ref[...]
    dt = jnp.maximum(z, 0.0) + jnp.log(1.0 + jnp.exp(-jnp.abs(z)))
    a = dt * (-jnp.exp(aneg_ref[...]))
    r = lax.broadcasted_iota(jnp.int32, (L, L), 0)
    c = lax.broadcasted_iota(jnp.int32, (L, L), 1)
    causal = (c >= r) if reverse else (c <= r)
    tri = jnp.where(causal, 1.0, 0.0).astype(BF16)
    acs = _dot_left01(tri, a)
    a_tot = jnp.sum(a, axis=0, keepdims=True)
    dte = jnp.exp(a_tot - acs)
    eacs = jnp.exp(acs)
    acs_t = acs.T

    dt_t = dt.T

    pk = 2 * SUBLANES
    eacs_hi, eacs_mid, _ = _split3(eacs)
    cdec_parts = [jnp.broadcast_to(part, (pk, LANES)) for part in _split3(jnp.exp(a_tot))]
    stack = jnp.concatenate([(dt * dte).astype(BF16), eacs_hi, eacs_mid] + cdec_parts, axis=0)
    ex = jnp.dot(stack, e_ref[...], preferred_element_type=F32)
    dtdte_e = ex[0:L]
    eacs_e = ex[L:2 * L] + ex[2 * L:3 * L]
    cdec_e = ex[3 * L:3 * L + 1] + ex[3 * L + pk:3 * L + pk + 1] + ex[3 * L + 2 * pk:3 * L + 2 * pk + 1]

    xw = (x_ref[:, :inner].astype(F32) * dtdte_e).astype(BF16)
    lane = lax.broadcasted_iota(jnp.int32, (L, LANES), 1)
    lo = lane < SSD_HEAD_DIM
    nt = (((1,), (1,)), ((), ()))
    tn = (((0,), (0,)), ((), ()))
    for g in range(SSD_GROUPS):
        Bg = x_ref[:, inner + g * SSD_STATE:inner + (g + 1) * SSD_STATE]
        Cg = x_ref[:, inner + gn + g * SSD_STATE:inner + gn + (g + 1) * SSD_STATE]
        gs = slice(g * gw, (g + 1) * gw)
        cb = lax.dot_general(Cg, Bg, nt, preferred_element_type=F32)
        s_prev = s_ref[:, gs]
        y_off = jnp.dot(Cg, s_prev.astype(BF16), preferred_element_type=F32) * eacs_e[:, gs]
        for k in range(hpg // 2):
            sl = slice(g * gw + k * LANES, g * gw + (k + 1) * LANES)
            halves = []
            for eh in range(2):
                col = head0 + g * hpg + 2 * k + eh
                seg = acs[:, col:col + 1] - acs_t[col:col + 1, :]
                w = cb * jnp.exp(jnp.where(causal, seg, NEG_INF)) * dt_t[col:col + 1, :]
                halves.append(jnp.dot(w.astype(BF16), x_ref[:, sl], preferred_element_type=F32))
            y = jnp.where(lo, halves[0], halves[1]) + y_off[:, k * LANES:(k + 1) * LANES]
            y_ref[:, sl] = y.astype(y_ref.dtype)
        s_ref[:, gs] = s_prev * cdec_e[:, gs] + lax.dot_general(Bg, xw[:, gs], tn, preferred_element_type=F32)


def ssd_scan(B, S, Lc, xbc, dt_raw, a_log_pad, dt_bias_pad, inner, n_heads):
    T, C = xbc.shape
    L = SSD_CHUNK
    nl, nc = S // L, Lc // L
    nsteps = nl + nc

    def fwd_idx(b, i):
        return (jnp.where(i < nc, B * nl + b * nc + i, b * nl + (i - nc)), 0)

    def bwd_idx(b, i):
        return (jnp.where(i < nc, B * nl + b * nc + (nc - 1 - i), b * nl + (nl - 1 - (i - nc))), 0)

    hd = np.arange(inner) // SSD_HEAD_DIM
    ef = (np.arange(LANES)[:, None] == hd[None, :]).astype(np.float32)
    eb = (np.arange(LANES)[:, None] == (hd[None, :] + n_heads)).astype(np.float32)
    kern = functools.partial(_ssd_scan_kernel, inner=inner, n_heads=n_heads)
    const = lambda shape: pl.BlockSpec(shape, lambda b, i: (0, 0))
    return pl.pallas_call(
        kern, grid=(B, nsteps),
        in_specs=[pl.BlockSpec((L, C), fwd_idx), pl.BlockSpec((L, C), bwd_idx),
                  pl.BlockSpec((L, LANES), fwd_idx), pl.BlockSpec((L, LANES), bwd_idx),
                  const((1, LANES)), const((1, LANES)), const((LANES, inner)), const((LANES, inner))],
        out_specs=[pl.BlockSpec((L, inner), fwd_idx), pl.BlockSpec((L, inner), bwd_idx)],
        out_shape=[jax.ShapeDtypeStruct((T, inner), BF16)] * 2,
        scratch_shapes=[pltpu.VMEM((SSD_STATE, inner), F32)] * 2,
        compiler_params=_cparams(("parallel", "arbitrary")), name="ssd_scan",
    )(xbc, xbc, dt_raw, dt_raw, a_log_pad, dt_bias_pad, jnp.asarray(ef, BF16), jnp.asarray(eb, BF16))


def _rope_tables(S, tm):
    n_rows = S // GRID_W
    rows = jnp.repeat(jnp.arange(n_rows), GRID_W)
    cols = jnp.tile(jnp.arange(GRID_W), n_rows)
    n_freq = HEAD_DIM // 4
    freqs = ROPE_THETA ** (-jnp.arange(n_freq, dtype=F32) / n_freq)
    ang = jnp.concatenate([rows[:, None].astype(F32) * freqs, cols[:, None].astype(F32) * freqs], axis=-1)
    cos, sin = jnp.cos(ang), jnp.sin(ang)
    cos_t = jnp.concatenate([cos, cos, cos, cos], axis=-1)
    sin_t = jnp.concatenate([-sin, sin, -sin, sin], axis=-1)
    cos_t = jnp.concatenate([cos_t, jnp.ones((tm, LANES), F32)], axis=0)
    sin_t = jnp.concatenate([sin_t, jnp.zeros((tm, LANES), F32)], axis=0)
    return cos_t, sin_t


def kernel(x, c, ctx, c_ctx, ada_w, ada_b, norm_g, ffn_w_in, ffn_w_out, da_w_qkv, da_w_o, da_lambda, da_subln,
           sw_w_qkv, sw_w_o, sw_sink, ssd_w_in, ssd_conv_w, ssd_conv_b, ssd_a_log, ssd_dt_bias, ssd_d_skip,
           ssd_norm, ssd_w_out):
    B, S, D = x.shape
    Lc = ctx.shape[1]
    depth = ada_w.shape[0]
    tm = min(512, math.gcd(S, B * Lc))
    lay = Layout(B, S, Lc, tm)

    n_cond = -(-(B + 1) // SUBLANES) * SUBLANES
    cond = jnp.concatenate([c, c_ctx[None, :], jnp.zeros((n_cond - B - 1, D), F32)], axis=0)
    mods = ada_modulation(cond, ada_w, ada_b)
    rope = _rope_tables(S, tm)
    lay_kv = Layout(B, S, Lc, min(256, math.gcd(S, Lc)))
    rope_kv = _rope_tables(S, lay_kv.tm)

    h = jnp.concatenate([x.reshape(B * S, D), ctx.reshape(B * Lc, D)], axis=0)
    da_width = DA_HEADS * 2 * HEAD_DIM
    sw_qw = SW_HEADS * HEAD_DIM
    sw_kvw = SW_KV_HEADS * HEAD_DIM
    perm = _sw_head_perm()

    for i in range(depth):
        want_ctx = i < depth - 1
        j = i // N_MIXERS
        kind = i % N_MIXERS
        mod = mods[i]
        g = norm_g[i]
        nblk = None if want_ctx else lay.nlat
        if kind == 0:
            lam_init = 0.8 - 0.6 * math.exp(-0.3 * i)
            w = da_w_qkv[j]
            q_scale = HEAD_DIM ** -0.5 * math.log2(math.e)
            w = jnp.concatenate([w[:, :da_width] * q_scale, w[:, da_width:]], axis=1).astype(BF16)
            n_keys = S + Lc
            kv_rows = (B * n_keys, lay_kv.kv_block, False)
            q, k_all, vt_all = in_proj(
                lay_kv, h, mod, g[0], w, [(0, da_width), (da_width, da_width), (2 * da_width, da_width)],
                [BF16] * 3, rope=rope_kv, rope_cols=2 * da_width,
                out_rows=[None, kv_rows, (B * n_keys, lay_kv.kv_block, True)], name="da_in_proj")
            o = da_attention(B, S, Lc, q, k_all, vt_all, da_lambda[j], da_subln[j], lam_init, want_ctx)
            mixer = (o, da_w_o[j].astype(BF16))
        elif kind == 1:
            w = sw_w_qkv[j]
            w = jnp.concatenate([w[:, :sw_qw][:, perm] * HEAD_DIM ** -0.5, w[:, sw_qw:]], axis=1).astype(BF16)
            (qkv,) = in_proj(lay, h, mod, g[0], w, [(0, sw_qw + 2 * sw_kvw)], [BF16], rope=rope,
                             rope_cols=sw_qw + sw_kvw, cw=256, name="sw_in_proj")
            o = sw_attention(B, S, Lc, qkv, sw_sink[j], want_ctx)
            mixer = (o, sw_w_o[j][perm, :].astype(BF16))
        else:
            n_heads = ssd_a_log.shape[2]
            inner = n_heads * SSD_HEAD_DIM
            conv_ch = inner + 2 * SSD_GROUPS * SSD_STATE
            w = ssd_w_in[j]
            w = jnp.pad(w, ((0, 0), (0, LANES - 2 * n_heads))).astype(BF16)
            z, xbc, dt_raw = in_proj(lay, h, mod, g[0], w,
                                     [(0, inner), (inner, conv_ch), (inner + conv_ch, LANES)],
                                     [BF16, BF16, F32], name="ssd_in_proj")
            xbc = ssd_conv(B, S, Lc, xbc, ssd_conv_w[j], ssd_conv_b[j])
            pad = lambda t: jnp.pad(t.reshape(1, 2 * n_heads), ((0, 0), (0, LANES - 2 * n_heads)))
            yf, yb = ssd_scan(B, S, Lc, xbc, dt_raw, pad(ssd_a_log[j]), pad(ssd_dt_bias[j]), inner, n_heads)
            dskip = jnp.repeat(ssd_d_skip[j], SSD_HEAD_DIM).reshape(1, inner)
            h = out_proj(lay, [yf, yb, xbc, z], ssd_w_out[j].astype(BF16), h, mod, g[1],
                         ssd_extra=(dskip, ssd_norm[j].reshape(1, inner)), nblk=nblk, name="ssd_out_proj")
            mixer = None
        h = ffn(lay, h, mod, g, ffn_w_in[i].astype(BF16), ffn_w_out[i].astype(BF16), mixer=mixer, nblk=nblk)
    return h[:B * S].reshape(B, S, D)
```

```python
import functools
import math

import numpy as np
import jax
import jax.numpy as jnp
from jax import lax
from jax.experimental import pallas as pl
from jax.experimental.pallas import tpu as pltpu

F32 = jnp.float32
BF16 = jnp.bfloat16

NORM_EPS = 1e-6
ROPE_THETA = 10000.0
NEG_INF = -1e30
GRID_W = 64
N_MOD = 6
N_MIXERS = 3

LANES = 128
SUBLANES = 8
VMEM_LIMIT = 56 * 1024 * 1024

HEAD_DIM = 64
DA_HEADS = 8
SW_HEADS = 16
SW_KV_HEADS = 4
SW_GROUP = SW_HEADS // SW_KV_HEADS
SW_WINDOW = 128
SW_BLOCK = 128
SSD_HEAD_DIM = 64
SSD_GROUPS = 4
SSD_STATE = 128
SSD_CONV = 5
SSD_CHUNK = 128
DA_TQ = 256
DA_TK = (256,)
SUM_ROWS = 16
DA_TILES = 1
DA_UNROLL = 16
CONV_HALO = 8


def _cparams(sem):
    return pltpu.CompilerParams(dimension_semantics=sem, vmem_limit_bytes=VMEM_LIMIT)


def _resident(shape):
    nd = len(shape)
    return pl.BlockSpec(shape, lambda *_: (0,) * nd, pipeline_mode=pl.Buffered(1))


def _rms(t):
    return t * lax.rsqrt(jnp.mean(t * t, axis=-1, keepdims=True) + NORM_EPS)


def _silu(t):
    return t * jax.nn.sigmoid(t)


def _ada_kernel(c_ref, w_ref, b_ref, o_ref):
    act = _silu(c_ref[...])
    o_ref[...] = jnp.dot(act, w_ref[...], precision=lax.Precision.HIGHEST,
                         preferred_element_type=F32) + b_ref[...]


def ada_modulation(cond, ada_w, ada_b):
    L, D, N = ada_w.shape
    R = cond.shape[0]
    tn = 1536
    out = pl.pallas_call(
        _ada_kernel,
        grid=(L, N // tn),
        in_specs=[pl.BlockSpec((R, D), lambda l, j: (0, 0)),
                  pl.BlockSpec((None, D, tn), lambda l, j: (l, 0, j)),
                  pl.BlockSpec((None, 1, tn), lambda l, j: (l, 0, j))],
        out_specs=pl.BlockSpec((None, R, tn), lambda l, j: (l, 0, j)),
        out_shape=jax.ShapeDtypeStruct((L, R, N), F32),
        compiler_params=_cparams(("arbitrary", "arbitrary")),
        name="ada_modulation",
    )(cond, ada_w, ada_b.reshape(L, 1, N))
    return out.reshape(L, R, N_MOD, D)


class Layout:
    def __init__(self, B, S, Lc, tm):
        assert S % tm == 0 and (B * Lc) % tm == 0
        self.B, self.S, self.Lc, self.tm = B, S, Lc, tm
        self.T = B * S + B * Lc
        self.npb = S // tm
        self.nlat = B * self.npb
        self.nblk = self.T // tm

    def mod_idx(self, i):
        return jnp.minimum(i // self.npb, self.B)

    def pos_idx(self, i):
        return jnp.where(i < self.nlat, i % self.npb, self.npb)

    def kv_block(self, i):
        per_batch = (self.S + self.Lc) // self.tm
        ncb = self.Lc // self.tm
        ci = i - self.nlat
        return jnp.where(i < self.nlat, (i // self.npb) * per_batch + i % self.npb,
                         (ci // ncb) * per_batch + self.npb + ci % ncb)


def _rope_slab(y, cos, sin_signed, lane):
    fwd = pltpu.roll(y, LANES - HEAD_DIM // 2, 1)
    bwd = pltpu.roll(y, HEAD_DIM // 2, 1)
    rot = jnp.where(lane % HEAD_DIM < HEAD_DIM // 2, fwd, bwd)
    return y * cos + rot * sin_signed


def _in_proj_kernel(*refs, segs, transposed, rope_cols, shift_row, scale_row, cw):
    h_ref, mod_ref, g_ref, w_ref = refs[:4]
    k = 4
    if rope_cols:
        cos_ref, sin_ref = refs[4:6]
        k = 6
    out_refs = refs[k:]
    h = h_ref[...]
    u = _rms(h) * g_ref[...]
    u = u * (1.0 + mod_ref[scale_row:scale_row + 1, :]) + mod_ref[shift_row:shift_row + 1, :]
    ub = u.astype(BF16)
    if rope_cols:
        cos = cos_ref[...]
        sin = sin_ref[...]
        lane = lax.broadcasted_iota(jnp.int32, cos.shape, 1)
    for o_ref, (col0, width), tr in zip(out_refs, segs, transposed):
        for c0 in range(0, width, cw):
            w_c = min(cw, width - c0)
            y = jnp.dot(ub, w_ref[:, col0 + c0:col0 + c0 + w_c], preferred_element_type=F32)
            if col0 + c0 < rope_cols:
                assert col0 + c0 + w_c <= rope_cols and w_c % LANES == 0
                y = jnp.concatenate(
                    [_rope_slab(y[:, s:s + LANES], cos, sin, lane) for s in range(0, w_c, LANES)], axis=1)
            if tr:
                o_ref[c0:c0 + w_c, :] = y.T.astype(o_ref.dtype)
            else:
                o_ref[:, c0:c0 + w_c] = y.astype(o_ref.dtype)


def in_proj(lay, h, mod, g, w, segs, out_dtypes, *, rope=None, rope_cols=0, shift_row=0, scale_row=1,
            cw=512, out_rows=None, name="in_proj"):
    tm, D = lay.tm, h.shape[1]
    N = w.shape[1]
    out_rows = out_rows or [None] * len(segs)
    out_rows = [r or (lay.T, lambda i: i, False) for r in out_rows]
    in_specs = [pl.BlockSpec((tm, D), lambda i: (i, 0)),
                pl.BlockSpec((None, N_MOD, D), lambda i: (lay.mod_idx(i), 0, 0)),
                pl.BlockSpec((1, D), lambda i: (0, 0)),
                _resident((D, N))]
    args = [h, mod, g.reshape(1, D), w]
    if rope_cols:
        cos, sin = rope
        in_specs += [pl.BlockSpec((tm, LANES), lambda i: (lay.pos_idx(i), 0))] * 2
        args += [cos, sin]
    out_specs, out_shape = [], []
    for (_, width), dt, (n_rows, blk, tr) in zip(segs, out_dtypes, out_rows):
        if tr:
            out_specs.append(pl.BlockSpec((width, tm), lambda i, blk=blk: (0, blk(i))))
            out_shape.append(jax.ShapeDtypeStruct((width, n_rows), dt))
        else:
            out_specs.append(pl.BlockSpec((tm, width), lambda i, blk=blk: (blk(i), 0)))
            out_shape.append(jax.ShapeDtypeStruct((n_rows, width), dt))
    kern = functools.partial(_in_proj_kernel, segs=tuple(segs), transposed=tuple(r[2] for r in out_rows),
                             rope_cols=rope_cols, shift_row=shift_row, scale_row=scale_row, cw=cw)
    return pl.pallas_call(
        kern, grid=(lay.nblk,), in_specs=in_specs, out_specs=out_specs, out_shape=out_shape,
        compiler_params=_cparams(("parallel",)), name=name,
    )(*args)


def _out_proj_kernel(*refs, ssd, gate_row):
    if ssd:
        yf_ref, yb_ref, xbc_ref, z_ref, dskip_ref, gn_ref, w_ref, h_ref, mod_ref, g_ref, o_ref = refs
        inner = yf_ref.shape[1]
        gw = inner // SSD_GROUPS
        y = (yf_ref[...].astype(F32) + yb_ref[...].astype(F32)
             + xbc_ref[:, :inner].astype(F32) * dskip_ref[...])
        y = y * _silu(z_ref[...].astype(F32))
        y = jnp.concatenate([_rms(y[:, k * gw:(k + 1) * gw]) for k in range(SSD_GROUPS)], axis=1)
        a = (y * gn_ref[...]).astype(BF16)
    else:
        a_ref, w_ref, h_ref, mod_ref, g_ref, o_ref = refs
        a = a_ref[...]
    y = jnp.dot(a, w_ref[...], preferred_element_type=F32)
    o_ref[...] = h_ref[...] + mod_ref[gate_row:gate_row + 1, :] * (_rms(y) * g_ref[...])


def out_proj(lay, acts, w, h, mod, g, *, ssd_extra=None, gate_row=2, nblk=None, name="out_proj"):
    tm, D = lay.tm, h.shape[1]
    K = w.shape[0]
    nblk = lay.nblk if nblk is None else nblk
    row = lambda width: pl.BlockSpec((tm, width), lambda i: (i, 0))
    in_specs = [row(a.shape[1]) for a in acts]
    args = list(acts)
    if ssd_extra is not None:
        dskip, gn = ssd_extra
        in_specs += [pl.BlockSpec((1, K), lambda i: (0, 0))] * 2
        args += [dskip, gn]
    in_specs += [_resident((K, D)), row(D),
                 pl.BlockSpec((None, N_MOD, D), lambda i: (lay.mod_idx(i), 0, 0)),
                 pl.BlockSpec((1, D), lambda i: (0, 0))]
    args += [w, h, mod, g.reshape(1, D)]
    kern = functools.partial(_out_proj_kernel, ssd=ssd_extra is not None, gate_row=gate_row)
    return pl.pallas_call(
        kern, grid=(nblk,), in_specs=in_specs, out_specs=row(D),
        out_shape=jax.ShapeDtypeStruct((nblk * tm, D), F32),
        compiler_params=_cparams(("parallel",)), name=name,
    )(*args)


def _ffn_kernel(*refs, hidden, cw, with_mixer):
    if with_mixer:
        a_ref, wo_ref, h_ref, mod_ref, g_ref, w_in_ref, w_out_ref, o_ref, act_ref = refs
        y = jnp.dot(a_ref[...], wo_ref[...], preferred_element_type=F32)
        h = h_ref[...] + mod_ref[2:3, :] * (_rms(y) * g_ref[1:2, :])
    else:
        h_ref, mod_ref, g_ref, w_in_ref, w_out_ref, o_ref, act_ref = refs
        h = h_ref[...]
    u = _rms(h) * g_ref[2:3, :]
    ub = (u * (1.0 + mod_ref[4:5, :]) + mod_ref[3:4, :]).astype(BF16)
    for c0 in range(0, hidden, cw):
        gate = jnp.dot(ub, w_in_ref[:, c0:c0 + cw], preferred_element_type=F32)
        up = jnp.dot(ub, w_in_ref[:, hidden + c0:hidden + c0 + cw], preferred_element_type=F32)
        act_ref[:, c0:c0 + cw] = (_silu(gate) * up).astype(BF16)
    y = jnp.dot(act_ref[...], w_out_ref[...], preferred_element_type=F32)
    o_ref[...] = h + mod_ref[5:6, :] * (_rms(y) * g_ref[3:4, :])


def ffn(lay, h, mod, g4, w_in, w_out, *, mixer=None, nblk=None, name="ffn"):
    tm, D = lay.tm, h.shape[1]
    hidden = w_out.shape[0]
    nblk = lay.nblk if nblk is None else nblk
    cw = 256
    assert hidden % cw == 0
    row = lambda width: pl.BlockSpec((tm, width), lambda i: (i, 0))
    in_specs, args = [], []
    if mixer is not None:
        a, w_o = mixer
        in_specs += [row(a.shape[1]), _resident(w_o.shape)]
        args += [a, w_o]
    in_specs += [row(D), pl.BlockSpec((None, N_MOD, D), lambda i: (lay.mod_idx(i), 0, 0)),
                 pl.BlockSpec((4, D), lambda i: (0, 0)),
                 _resident((D, 2 * hidden)), _resident((hidden, D))]
    args += [h, mod, g4, w_in, w_out]
    kern = functools.partial(_ffn_kernel, hidden=hidden, cw=cw, with_mixer=mixer is not None)
    return pl.pallas_call(
        kern, grid=(nblk,), in_specs=in_specs, out_specs=row(D),
        out_shape=jax.ShapeDtypeStruct((nblk * tm, D), F32),
        scratch_shapes=[pltpu.VMEM((tm, hidden), BF16)],
        compiler_params=_cparams(("parallel",)), name=name,
    )(*args)


def _da_attn_kernel(lam_ref, subln_ref, q_ref, k_ref, vt_ref, o_ref, *scratch, lam_init, tk, n_chunks, n_tiles):
    tq = q_ref.shape[0] // n_tiles
    ns = 2 * n_tiles
    accs = scratch[:ns]
    s_bufs = (scratch[ns:2 * ns], scratch[2 * ns:3 * ns])
    p_bufs = (scratch[3 * ns:4 * ns], scratch[4 * ns:5 * ns])
    qts = []
    for t in range(n_tiles):
        qt = q_ref[t * tq:(t + 1) * tq, :].astype(F32).T
        row = lax.broadcasted_iota(jnp.int32, qt.shape, 0)
        qts += [jnp.where(row < HEAD_DIM, qt, 0.0).astype(BF16), jnp.where(row >= HEAD_DIM, qt, 0.0).astype(BF16)]

    def chunk(c):
        start = c * tk
        return pl.ds(start if isinstance(start, int) else pl.multiple_of(start, tk), tk)

    def scores(c, slot):
        kc = k_ref[chunk(c), :]
        for st in range(ns):
            s_bufs[slot][st][...] = jnp.dot(kc, qts[st], preferred_element_type=F32)

    def accumulate(c, slot, alphas):
        vt = jnp.concatenate([vt_ref[:, chunk(c)],
                              jnp.ones((SUM_ROWS, tk), BF16)], axis=0)
        for st in range(ns):
            accs[st][...] = alphas[st] * accs[st][...] + jnp.dot(vt, p_bufs[slot][st][...],
                                                                 preferred_element_type=F32)

    def softmax(slot, maxes):
        new, alphas = [], []
        for st in range(ns):
            s = s_bufs[slot][st][...]
            m_new = jnp.maximum(maxes[st], jnp.max(s, axis=0, keepdims=True))
            p_bufs[slot][st][...] = jnp.exp2(s - m_new).astype(BF16)
            new.append(m_new)
            alphas.append(jnp.exp2(maxes[st] - m_new))
        return tuple(new), tuple(alphas)

    for st in range(ns):
        accs[st][...] = jnp.zeros_like(accs[st])
    scores(0, 0)
    stats = (jnp.full((1, tq), NEG_INF, F32),) * ns
    assert n_chunks % 2 == 1 and k_ref.shape[0] == n_chunks * tk

    n_pairs = (n_chunks - 1) // 2
    unroll = math.gcd(n_pairs, max(DA_UNROLL // n_tiles, 1)) if n_pairs else 1

    def pairs(t, carry, first_has_no_predecessor=False):
        stats, alphas1 = carry
        for u in range(unroll):
            c = 2 * (t * unroll + u)
            scores(c + 1, 1)
            if not (first_has_no_predecessor and u == 0):
                accumulate(max(c - 1, 0) if isinstance(c, int) else jnp.maximum(c - 1, 0), 1, alphas1)
            stats, alphas0 = softmax(0, stats)
            scores(c + 2, 0)
            accumulate(c, 0, alphas0)
            stats, alphas1 = softmax(1, stats)
        return stats, alphas1

    if n_pairs == unroll:
        stats, alphas1 = pairs(0, (stats, None), first_has_no_predecessor=True)
    else:
        for st in range(ns):
            p_bufs[1][st][...] = jnp.zeros_like(p_bufs[1][st])
        ones = (jnp.ones((1, tq), F32),) * ns
        stats, alphas1 = lax.fori_loop(0, n_pairs // unroll, pairs, (stats, ones))
    last = n_chunks - 1
    accumulate(max(last - 1, 0), 1, alphas1)
    stats, alphas0 = softmax(0, stats)
    accumulate(last, 0, alphas0)

    lp = lam_ref[...]
    lam = (jnp.exp(jnp.sum(lp[0:1] * lp[1:2], axis=-1, keepdims=True))
           - jnp.exp(jnp.sum(lp[2:3] * lp[3:4], axis=-1, keepdims=True)) + lam_init)
    for t in range(n_tiles):
        a0, a1 = accs[2 * t][...], accs[2 * t + 1][...]
        ot = a0[:LANES] / a0[LANES:LANES + 1] - lam * (a1[:LANES] / a1[LANES:LANES + 1])
        o = _rms(ot.T) * subln_ref[...] * (1.0 - lam_init)
        o_ref[t * tq:(t + 1) * tq, :] = o.astype(o_ref.dtype)


def da_attention(B, S, Lc, q_src, k_all, vt_all, lam_p, subln, lam_init, want_ctx):
    T = q_src.shape[0]
    width = DA_HEADS * 2 * HEAD_DIM
    n_keys = Lc + S
    small = (jnp.asarray(lam_p), subln.reshape(1, 2 * HEAD_DIM))
    small_specs = [pl.BlockSpec((4, HEAD_DIM), lambda b, h, i: (0, 0)),
                   pl.BlockSpec((1, 2 * HEAD_DIM), lambda b, h, i: (0, 0))]

    def call(tq, n_tiles, tk, n_rows_k, k_blocks_per_batch, k_block0, nq, q_block0, prev):
        n_chunks = n_rows_k // tk
        ns = 2 * n_tiles
        kern = functools.partial(_da_attn_kernel, lam_init=lam_init, tk=tk, n_chunks=n_chunks, n_tiles=n_tiles)
        q_idx = lambda b, h, i: (q_block0 + b * nq + i, h)
        in_specs = small_specs + [pl.BlockSpec((n_tiles * tq, LANES), q_idx),
                                  pl.BlockSpec((n_rows_k, LANES),
                                               lambda b, h, i: (b * k_blocks_per_batch + k_block0, h)),
                                  pl.BlockSpec((LANES, n_rows_k),
                                               lambda b, h, i: (h, b * k_blocks_per_batch + k_block0))]
        args = small + (q_src, k_all, vt_all)
        aliases = {}
        body = kern
        if prev is not None:
            in_specs.append(pl.BlockSpec(memory_space=pl.ANY))
            args += (prev,)
            aliases = {len(args) - 1: 0}
            body = lambda *refs: kern(*refs[:5], *refs[6:])
        return pl.pallas_call(
            body, grid=(B, DA_HEADS, nq), in_specs=in_specs,
            out_specs=pl.BlockSpec((n_tiles * tq, LANES), q_idx),
            out_shape=jax.ShapeDtypeStruct((T, width), BF16),
            scratch_shapes=([pltpu.VMEM((LANES + SUM_ROWS, tq), F32)] * ns + [pltpu.VMEM((tk, tq), F32)] * (2 * ns)
                            + [pltpu.VMEM((tk, tq), BF16)] * (2 * ns)),
            input_output_aliases=aliases,
            compiler_params=_cparams(("parallel", "parallel", "arbitrary")),
            name="da_attention" if prev is None else "da_attention_ctx",
        )(*args)

    tq = min(DA_TQ, S)
    n_tiles = DA_TILES if S % (DA_TILES * tq) == 0 else 1
    tk = next(t for t in DA_TK if n_keys % t == 0 and (n_keys // t) % 2 == 1)
    o = call(tq, n_tiles, tk, n_keys, 1, 0, S // (n_tiles * tq), 0, None)
    if want_ctx:
        assert S % Lc == 0
        o = call(Lc, 1, Lc, Lc, n_keys // Lc, S // Lc, 1, B * S // Lc, o)
    return o


def _sw_attn_kernel(sink_ref, q_ref, kc_ref, vc_ref, kp_ref, kx_ref, kn_ref, vp_ref, vx_ref, vn_ref,
                    o_ref, *, n_lat_q, nb):
    tq = q_ref.shape[0]
    Lc = kc_ref.shape[0]
    i = pl.program_id(1)
    is_lat = i < n_lat_q
    n = i % nb
    krel = lax.broadcasted_iota(jnp.int32, (3 * SW_BLOCK, tq), 0) - SW_BLOCK
    qrow = lax.broadcasted_iota(jnp.int32, (3 * SW_BLOCK, tq), 1)
    kabs = krel + n * SW_BLOCK
    valid = (jnp.abs(qrow - krel) <= SW_WINDOW) & (kabs >= 0) & (kabs < nb * SW_BLOCK) & is_lat
    valid = jnp.concatenate([valid] * SW_GROUP, axis=1)
    row = lax.broadcasted_iota(jnp.int32, (LANES, SW_GROUP * tq), 0)
    lo = row < HEAD_DIM
    for j in range(SW_KV_HEADS // 2):
        ksl = slice(j * LANES, (j + 1) * LANES)
        keys = jnp.concatenate([kc_ref[:, ksl], kp_ref[:, ksl], kx_ref[:, ksl], kn_ref[:, ksl]], axis=0)
        vals = jnp.concatenate([vc_ref[:, ksl], vp_ref[:, ksl], vx_ref[:, ksl], vn_ref[:, ksl]], axis=0)
        vals_t = jnp.concatenate([vals.astype(F32).T.astype(BF16),
                                  jnp.ones((SUM_ROWS, vals.shape[0]), BF16)], axis=0)
        qt = jnp.concatenate(
            [q_ref[:, (j * SW_GROUP + g) * LANES:(j * SW_GROUP + g + 1) * LANES].astype(F32).T
             for g in range(SW_GROUP)], axis=1)
        halves = []
        for e in range(2):
            sink = sink_ref[2 * j + e:2 * j + e + 1, :]
            qz = jnp.where(lo if e == 0 else ~lo, qt, 0.0).astype(BF16)
            s = jnp.dot(keys, qz, preferred_element_type=F32)
            s_ctx = s[:Lc]
            s_loc = jnp.where(valid, s[Lc:], NEG_INF)
            m = jnp.maximum(jnp.maximum(jnp.max(s_ctx, axis=0, keepdims=True),
                                        jnp.max(s_loc, axis=0, keepdims=True)), sink)
            p = jnp.concatenate([jnp.exp(s_ctx - m), jnp.exp(s_loc - m)], axis=0).astype(BF16)
            pv = jnp.dot(vals_t, p, preferred_element_type=F32)
            denom = pv[LANES:LANES + 1] + jnp.exp(sink - m)
            halves.append(pv[:LANES] / denom)
        ot = jnp.where(lo, halves[0], halves[1])
        for g in range(SW_GROUP):
            slab = j * SW_GROUP + g
            o_ref[:, slab * LANES:(slab + 1) * LANES] = ot[:, g * tq:(g + 1) * tq].T.astype(o_ref.dtype)


def sw_attention(B, S, Lc, qkv, sink, want_ctx):
    T = qkv.shape[0]
    qw = SW_HEADS * HEAD_DIM
    kvw = SW_KV_HEADS * HEAD_DIM
    tq = SW_BLOCK
    nb = S // SW_BLOCK
    nqc = Lc // tq
    nq = nb + (nqc if want_ctx else 0)
    kcol, vcol = qw // kvw, qw // kvw + 1

    def q_idx(b, i):
        return (jnp.where(i < nb, b * nb + i, B * nb + b * nqc + (i - nb)), 0)

    def band(col, off):
        def idx(b, i):
            n = jnp.clip(jnp.where(i < nb, i, 0) + off, 0, nb - 1)
            return (b * nb + n, col)
        return pl.BlockSpec((SW_BLOCK, kvw), idx)

    sink_lanes = jnp.repeat(sink.astype(F32).reshape(SW_KV_HEADS, SW_GROUP), tq, axis=1)
    kern = functools.partial(_sw_attn_kernel, n_lat_q=nb, nb=nb)
    return pl.pallas_call(
        kern, grid=(B, nq),
        in_specs=[pl.BlockSpec((SW_KV_HEADS, SW_GROUP * tq), lambda b, i: (0, 0)),
                  pl.BlockSpec((tq, qw), q_idx),
                  pl.BlockSpec((Lc, kvw), lambda b, i: (B * S // Lc + b, kcol)),
                  pl.BlockSpec((Lc, kvw), lambda b, i: (B * S // Lc + b, vcol)),
                  band(kcol, -1), band(kcol, 0), band(kcol, 1),
                  band(vcol, -1), band(vcol, 0), band(vcol, 1)],
        out_specs=pl.BlockSpec((tq, qw), q_idx),
        out_shape=jax.ShapeDtypeStruct((T, qw), BF16),
        compiler_params=_cparams(("parallel", "arbitrary")), name="sw_attention",
    )(sink_lanes, qkv, qkv, qkv, qkv, qkv, qkv, qkv, qkv, qkv)


def _sw_head_perm():
    cols = []
    for j in range(SW_KV_HEADS // 2):
        for g in range(SW_GROUP):
            for e in range(2):
                head = (2 * j + e) * SW_GROUP + g
                cols += list(range(head * HEAD_DIM, (head + 1) * HEAD_DIM))
    return np.asarray(cols, dtype=np.int32)


def _ssd_conv_kernel(prev_ref, cur_ref, next_ref, w_ref, b_ref, o_ref, ext_ref, *, seg_lat, seg_ctx, n_lat_blk):
    tc = cur_ref.shape[0]
    i = pl.program_id(0)
    row0 = i * tc
    is_lat = i < n_lat_blk
    seg = jnp.where(is_lat, seg_lat, seg_ctx)
    rel = jnp.where(is_lat, row0, row0 - n_lat_blk * tc) % seg
    keep_prev = (rel != 0).astype(F32)
    keep_next = (rel + tc != seg).astype(F32)
    ext_ref[0:CONV_HALO, :] = prev_ref[...].astype(F32) * keep_prev
    ext_ref[CONV_HALO:CONV_HALO + tc, :] = cur_ref[...].astype(F32)
    ext_ref[CONV_HALO + tc:, :] = next_ref[...].astype(F32) * keep_next
    acc = jnp.zeros(o_ref.shape, F32) + b_ref[...]
    for k in range(SSD_CONV):
        off = CONV_HALO - SSD_CONV // 2 + k
        acc = acc + ext_ref[off:off + tc, :] * w_ref[k:k + 1, :]
    o_ref[...] = _silu(acc).astype(o_ref.dtype)


def ssd_conv(B, S, Lc, xbc, conv_w, conv_b):
    T, C = xbc.shape
    tc = min(256, Lc)
    cwid = next(w for w in (1536, 1024, 512) if C % w == 0)
    nblk = T // tc
    hb = tc // CONV_HALO
    nh = T // CONV_HALO
    kern = functools.partial(_ssd_conv_kernel, seg_lat=S, seg_ctx=Lc, n_lat_blk=B * S // tc)
    return pl.pallas_call(
        kern, grid=(nblk, C // cwid),
        in_specs=[pl.BlockSpec((CONV_HALO, cwid), lambda i, j: (jnp.maximum(i * hb - 1, 0), j)),
                  pl.BlockSpec((tc, cwid), lambda i, j: (i, j)),
                  pl.BlockSpec((CONV_HALO, cwid), lambda i, j: (jnp.minimum((i + 1) * hb, nh - 1), j)),
                  pl.BlockSpec((SSD_CONV, cwid), lambda i, j: (0, j)),
                  pl.BlockSpec((1, cwid), lambda i, j: (0, j))],
        out_specs=pl.BlockSpec((tc, cwid), lambda i, j: (i, j)),
        out_shape=jax.ShapeDtypeStruct((T, C), BF16),
        scratch_shapes=[pltpu.VMEM((tc + 2 * CONV_HALO, cwid), F32)],
        compiler_params=_cparams(("parallel", "parallel")), name="ssd_conv",
    )(xbc, xbc, xbc, conv_w, conv_b.reshape(1, C))


def _split3(t):
    hi = t.astype(BF16)
    r1 = t - hi.astype(F32)
    mid = r1.astype(BF16)
    lo = (r1 - mid.astype(F32)).astype(BF16)
    return hi, mid, lo


def _dot01(t, ones_mat):
    out = None
    for part in _split3(t):
        term = jnp.dot(part, ones_mat, preferred_element_type=F32)
        out = term if out is None else out + term
    return out


def _dot_left01(ones_mat, t):
    out = None
    for part in _split3(t):
        term = jnp.dot(ones_mat, part, preferred_element_type=F32)
        out = term if out is None else out + term
    return out


def _ssd_scan_kernel(xf_ref, xb_ref, dtf_ref, dtb_ref, aneg_ref, bias_ref, ef_ref, eb_ref,
                     yf_ref, yb_ref, sf_ref, sb_ref, *, inner, n_heads):
    @pl.when(pl.program_id(1) == 0)
    def _():
        sf_ref[...] = jnp.zeros_like(sf_ref)
        sb_ref[...] = jnp.zeros_like(sb_ref)

    for reverse, x_ref, dt_ref, e_ref, y_ref, s_ref in (
            (False, xf_ref, dtf_ref, ef_ref, yf_ref, sf_ref),
            (True, xb_ref, dtb_ref, eb_ref, yb_ref, sb_ref)):
        _ssd_chunk(x_ref, dt_ref, aneg_ref, bias_ref, e_ref, y_ref, s_ref,
                   reverse=reverse, inner=inner, n_heads=n_heads)


def _ssd_chunk(x_ref, dt_ref, aneg_ref, bias_ref, e_ref, y_ref, s_ref, *, reverse, inner, n_heads):
    L = x_ref.shape[0]
    gn = SSD_GROUPS * SSD_STATE
    hpg = n_heads // SSD_GROUPS
    gw = hpg * SSD_HEAD_DIM
    head0 = n_heads if reverse else 0

    z = dt_ref[...] + bias_ref[...]
    dt = jnp.maximum(z, 0.0) + jnp.log(1.0 + jnp.exp(-jnp.abs(z)))
    a = dt * (-jnp.exp(aneg_ref[...]))
    r = lax.broadcasted_iota(jnp.int32, (L, L), 0)
    c = lax.broadcasted_iota(jnp.int32, (L, L), 1)
    causal = (c >= r) if reverse else (c <= r)
    tri = jnp.where(causal, 1.0, 0.0).astype(BF16)
    acs = _dot_left01(tri, a)
    a_tot = jnp.sum(a, axis=0, keepdims=True)
    dte = jnp.exp(a_tot - acs)
    eacs = jnp.exp(acs)
    acs_t = acs.T

    dt_t = dt.T

    pk = 2 * SUBLANES
    eacs_hi, eacs_mid, _ = _split3(eacs)
    cdec_parts = [jnp.broadcast_to(part, (pk, LANES)) for part in _split3(jnp.exp(a_tot))]
    stack = jnp.concatenate([(dt * dte).astype(BF16), eacs_hi, eacs_mid] + cdec_parts, axis=0)
    ex = jnp.dot(stack, e_ref[...], preferred_element_type=F32)
    dtdte_e = ex[0:L]
    eacs_e = ex[L:2 * L] + ex[2 * L:3 * L]
    cdec_e = ex[3 * L:3 * L + 1] + ex[3 * L + pk:3 * L + pk + 1] + ex[3 * L + 2 * pk:3 * L + 2 * pk + 1]

    xw = (x_ref[:, :inner].astype(F32) * dtdte_e).astype(BF16)
    lane = lax.broadcasted_iota(jnp.int32, (L, LANES), 1)
    lo = lane < SSD_HEAD_DIM
    nt = (((1,), (1,)), ((), ()))
    tn = (((0,), (0,)), ((), ()))
    for g in range(SSD_GROUPS):
        Bg = x_ref[:, inner + g * SSD_STATE:inner + (g + 1) * SSD_STATE]
        Cg = x_ref[:, inner + gn + g * SSD_STATE:inner + gn + (g + 1) * SSD_STATE]
        gs = slice(g * gw, (g + 1) * gw)
        cb = lax.dot_general(Cg, Bg, nt, preferred_element_type=F32)
        s_prev = s_ref[:, gs]
        y_off = jnp.dot(Cg, s_prev.astype(BF16), preferred_element_type=F32) * eacs_e[:, gs]
        for k in range(hpg // 2):
            sl = slice(g * gw + k * LANES, g * gw + (k + 1) * LANES)
            halves = []
            for eh in range(2):
                col = head0 + g * hpg + 2 * k + eh
                seg = acs[:, col:col + 1] - acs_t[col:col + 1, :]
                w = cb * jnp.exp(jnp.where(causal, seg, NEG_INF)) * dt_t[col:col + 1, :]
                halves.append(jnp.dot(w.astype(BF16), x_ref[:, sl], preferred_element_type=F32))
            y = jnp.where(lo, halves[0], halves[1]) + y_off[:, k * LANES:(k + 1) * LANES]
            y_ref[:, sl] = y.astype(y_ref.dtype)
        s_ref[:, gs] = s_prev * cdec_e[:, gs] + lax.dot_general(Bg, xw[:, gs], tn, preferred_element_type=F32)


def ssd_scan(B, S, Lc, xbc, dt_raw, a_log_pad, dt_bias_pad, inner, n_heads):
    T, C = xbc.shape
    L = SSD_CHUNK
    nl, nc = S // L, Lc // L
    nsteps = nl + nc

    def fwd_idx(b, i):
        return (jnp.where(i < nc, B * nl + b * nc + i, b * nl + (i - nc)), 0)

    def bwd_idx(b, i):
        return (jnp.where(i < nc, B * nl + b * nc + (nc - 1 - i), b * nl + (nl - 1 - (i - nc))), 0)

    hd = np.arange(inner) // SSD_HEAD_DIM
    ef = (np.arange(LANES)[:, None] == hd[None, :]).astype(np.float32)
    eb = (np.arange(LANES)[:, None] == (hd[None, :] + n_heads)).astype(np.float32)
    kern = functools.partial(_ssd_scan_kernel, inner=inner, n_heads=n_heads)
    const = lambda shape: pl.BlockSpec(shape, lambda b, i: (0, 0))
    return pl.pallas_call(
        kern, grid=(B, nsteps),
        in_specs=[pl.BlockSpec((L, C), fwd_idx), pl.BlockSpec((L, C), bwd_idx),
                  pl.BlockSpec((L, LANES), fwd_idx), pl.BlockSpec((L, LANES), bwd_idx),
                  const((1, LANES)), const((1, LANES)), const((LANES, inner)), const((LANES, inner))],
        out_specs=[pl.BlockSpec((L, inner), fwd_idx), pl.BlockSpec((L, inner), bwd_idx)],
        out_shape=[jax.ShapeDtypeStruct((T, inner), BF16)] * 2,
        scratch_shapes=[pltpu.VMEM((SSD_STATE, inner), F32)] * 2,
        compiler_params=_cparams(("parallel", "arbitrary")), name="ssd_scan",
    )(xbc, xbc, dt_raw, dt_raw, a_log_pad, dt_bias_pad, jnp.asarray(ef, BF16), jnp.asarray(eb, BF16))


def _rope_tables(S, tm):
    n_rows = S // GRID_W
    rows = jnp.repeat(jnp.arange(n_rows), GRID_W)
    cols = jnp.tile(jnp.arange(GRID_W), n_rows)
    n_freq = HEAD_DIM // 4
    freqs = ROPE_THETA ** (-jnp.arange(n_freq, dtype=F32) / n_freq)
    ang = jnp.concatenate([rows[:, None].astype(F32) * freqs, cols[:, None].astype(F32) * freqs], axis=-1)
    cos, sin = jnp.cos(ang), jnp.sin(ang)
    cos_t = jnp.concatenate([cos, cos, cos, cos], axis=-1)
    sin_t = jnp.concatenate([-sin, sin, -sin, sin], axis=-1)
    cos_t = jnp.concatenate([cos_t, jnp.ones((tm, LANES), F32)], axis=0)
    sin_t = jnp.concatenate([sin_t, jnp.zeros((tm, LANES), F32)], axis=0)
    return cos_t, sin_t


def kernel(x, c, ctx, c_ctx, ada_w, ada_b, norm_g, ffn_w_in, ffn_w_out, da_w_qkv, da_w_o, da_lambda, da_subln,
           sw_w_qkv, sw_w_o, sw_sink, ssd_w_in, ssd_conv_w, ssd_conv_b, ssd_a_log, ssd_dt_bias, ssd_d_skip,
           ssd_norm, ssd_w_out):
    B, S, D = x.shape
    Lc = ctx.shape[1]
    depth = ada_w.shape[0]
    tm = min(512, math.gcd(S, B * Lc))
    lay = Layout(B, S, Lc, tm)

    n_cond = -(-(B + 1) // SUBLANES) * SUBLANES
    cond = jnp.concatenate([c, c_ctx[None, :], jnp.zeros((n_cond - B - 1, D), F32)], axis=0)
    mods = ada_modulation(cond, ada_w, ada_b)
    rope = _rope_tables(S, tm)
    lay_kv = Layout(B, S, Lc, min(256, math.gcd(S, Lc)))
    rope_kv = _rope_tables(S, lay_kv.tm)

    h = jnp.concatenate([x.reshape(B * S, D), ctx.reshape(B * Lc, D)], axis=0)
    da_width = DA_HEADS * 2 * HEAD_DIM
    sw_qw = SW_HEADS * HEAD_DIM
    sw_kvw = SW_KV_HEADS * HEAD_DIM
    perm = _sw_head_perm()

    for i in range(depth):
        want_ctx = i < depth - 1
        j = i // N_MIXERS
        kind = i % N_MIXERS
        mod = mods[i]
        g = norm_g[i]
        nblk = None if want_ctx else lay.nlat
        if kind == 0:
            lam_init = 0.8 - 0.6 * math.exp(-0.3 * i)
            w = da_w_qkv[j]
            q_scale = HEAD_DIM ** -0.5 * math.log2(math.e)
            w = jnp.concatenate([w[:, :da_width] * q_scale, w[:, da_width:]], axis=1).astype(BF16)
            n_keys = S + Lc
            kv_rows = (B * n_keys, lay_kv.kv_block, False)
            q, k_all, vt_all = in_proj(
                lay_kv, h, mod, g[0], w, [(0, da_width), (da_width, da_width), (2 * da_width, da_width)],
                [BF16] * 3, rope=rope_kv, rope_cols=2 * da_width,
                out_rows=[None, kv_rows, (B * n_keys, lay_kv.kv_block, True)], name="da_in_proj")
            o = da_attention(B, S, Lc, q, k_all, vt_all, da_lambda[j], da_subln[j], lam_init, want_ctx)
            mixer = (o, da_w_o[j].astype(BF16))
        elif kind == 1:
            w = sw_w_qkv[j]
            w = jnp.concatenate([w[:, :sw_qw][:, perm] * HEAD_DIM ** -0.5, w[:, sw_qw:]], axis=1).astype(BF16)
            (qkv,) = in_proj(lay, h, mod, g[0], w, [(0, sw_qw + 2 * sw_kvw)], [BF16], rope=rope,
                             rope_cols=sw_qw + sw_kvw, cw=256, name="sw_in_proj")
            o = sw_attention(B, S, Lc, qkv, sw_sink[j], want_ctx)
            mixer = (o, sw_w_o[j][perm, :].astype(BF16))
        else:
            n_heads = ssd_a_log.shape[2]
            inner = n_heads * SSD_HEAD_DIM
            conv_ch = inner + 2 * SSD_GROUPS * SSD_STATE
            w = ssd_w_in[j]
            w = jnp.pad(w, ((0, 0), (0, LANES - 2 * n_heads))).astype(BF16)
            z, xbc, dt_raw = in_proj(lay, h, mod, g[0], w,
                                     [(0, inner), (inner, conv_ch), (inner + conv_ch, LANES)],
                                     [BF16, BF16, F32], name="ssd_in_proj")
            xbc = ssd_conv(B, S, Lc, xbc, ssd_conv_w[j], ssd_conv_b[j])
            pad = lambda t: jnp.pad(t.reshape(1, 2 * n_heads), ((0, 0), (0, LANES - 2 * n_heads)))
            yf, yb = ssd_scan(B, S, Lc, xbc, dt_raw, pad(ssd_a_log[j]), pad(ssd_dt_bias[j]), inner, n_heads)
            dskip = jnp.repeat(ssd_d_skip[j], SSD_HEAD_DIM).reshape(1, inner)
            h = out_proj(lay, [yf, yb, xbc, z], ssd_w_out[j].astype(BF16), h, mod, g[1],
                         ssd_extra=(dskip, ssd_norm[j].reshape(1, inner)), nblk=nblk, name="ssd_out_proj")
            mixer = None
        h = ffn(lay, h, mod, g, ffn_w_in[i].astype(BF16), ffn_w_out[i].astype(BF16), mixer=mixer, nblk=nblk)
    return h[:B * S].reshape(B, S, D)
```

```python
import functools
import math

import numpy as np
import jax
import jax.numpy as jnp
from jax import lax
from jax.experimental import pallas as pl
from jax.experimental.pallas import tpu as pltpu

F32 = jnp.float32
BF16 = jnp.bfloat16

NORM_EPS = 1e-6
ROPE_THETA = 10000.0
NEG_INF = -1e30
GRID_W = 64
N_MOD = 6
N_MIXERS = 3

LANES = 128
SUBLANES = 8
VMEM_LIMIT = 56 * 1024 * 1024

HEAD_DIM = 64
DA_HEADS = 8
SW_HEADS = 16
SW_KV_HEADS = 4
SW_GROUP = SW_HEADS // SW_KV_HEADS
SW_WINDOW = 128
SW_BLOCK = 128
SSD_HEAD_DIM = 64
SSD_GROUPS = 4
SSD_STATE = 128
SSD_CONV = 5
SSD_CHUNK = 128
DA_TQ = 256
DA_TK = (256,)
DA_SUM_ROWS = 16
DA_TILES = 1
DA_UNROLL = 16
CONV_HALO = 8


def _cparams(sem):
    return pltpu.CompilerParams(dimension_semantics=sem, vmem_limit_bytes=VMEM_LIMIT)


def _resident(shape):
    nd = len(shape)
    return pl.BlockSpec(shape, lambda *_: (0,) * nd, pipeline_mode=pl.Buffered(1))


def _rms(t):
    return t * lax.rsqrt(jnp.mean(t * t, axis=-1, keepdims=True) + NORM_EPS)


def _silu(t):
    return t * jax.nn.sigmoid(t)


def _ada_kernel(c_ref, w_ref, b_ref, o_ref):
    act = _silu(c_ref[...])
    o_ref[...] = jnp.dot(act, w_ref[...], precision=lax.Precision.HIGHEST,
                         preferred_element_type=F32) + b_ref[...]


def ada_modulation(cond, ada_w, ada_b):
    L, D, N = ada_w.shape
    R = cond.shape[0]
    tn = 1536
    out = pl.pallas_call(
        _ada_kernel,
        grid=(L, N // tn),
        in_specs=[pl.BlockSpec((R, D), lambda l, j: (0, 0)),
                  pl.BlockSpec((None, D, tn), lambda l, j: (l, 0, j)),
                  pl.BlockSpec((None, 1, tn), lambda l, j: (l, 0, j))],
        out_specs=pl.BlockSpec((None, R, tn), lambda l, j: (l, 0, j)),
        out_shape=jax.ShapeDtypeStruct((L, R, N), F32),
        compiler_params=_cparams(("arbitrary", "arbitrary")),
        name="ada_modulation",
    )(cond, ada_w, ada_b.reshape(L, 1, N))
    return out.reshape(L, R, N_MOD, D)


class Layout:
    def __init__(self, B, S, Lc, tm):
        assert S % tm == 0 and (B * Lc) % tm == 0
        self.B, self.S, self.Lc, self.tm = B, S, Lc, tm
        self.T = B * S + B * Lc
        self.npb = S // tm
        self.nlat = B * self.npb
        self.nblk = self.T // tm

    def mod_idx(self, i):
        return jnp.minimum(i // self.npb, self.B)

    def pos_idx(self, i):
        return jnp.where(i < self.nlat, i % self.npb, self.npb)

    def kv_block(self, i):
        per_batch = (self.S + self.Lc) // self.tm
        ncb = self.Lc // self.tm
        ci = i - self.nlat
        return jnp.where(i < self.nlat, (i // self.npb) * per_batch + i % self.npb,
                         (ci // ncb) * per_batch + self.npb + ci % ncb)


def _rope_slab(y, cos, sin_signed, lane):
    fwd = pltpu.roll(y, LANES - HEAD_DIM // 2, 1)
    bwd = pltpu.roll(y, HEAD_DIM // 2, 1)
    rot = jnp.where(lane % HEAD_DIM < HEAD_DIM // 2, fwd, bwd)
    return y * cos + rot * sin_signed


def _in_proj_kernel(*refs, segs, transposed, rope_cols, shift_row, scale_row, cw):
    h_ref, mod_ref, g_ref, w_ref = refs[:4]
    k = 4
    if rope_cols:
        cos_ref, sin_ref = refs[4:6]
        k = 6
    out_refs = refs[k:]
    h = h_ref[...]
    u = _rms(h) * g_ref[...]
    u = u * (1.0 + mod_ref[scale_row:scale_row + 1, :]) + mod_ref[shift_row:shift_row + 1, :]
    ub = u.astype(BF16)
    if rope_cols:
        cos = cos_ref[...]
        sin = sin_ref[...]
        lane = lax.broadcasted_iota(jnp.int32, cos.shape, 1)
    for o_ref, (col0, width), tr in zip(out_refs, segs, transposed):
        for c0 in range(0, width, cw):
            w_c = min(cw, width - c0)
            y = jnp.dot(ub, w_ref[:, col0 + c0:col0 + c0 + w_c], preferred_element_type=F32)
            if col0 + c0 < rope_cols:
                assert col0 + c0 + w_c <= rope_cols and w_c % LANES == 0
                y = jnp.concatenate(
                    [_rope_slab(y[:, s:s + LANES], cos, sin, lane) for s in range(0, w_c, LANES)], axis=1)
            if tr:
                o_ref[c0:c0 + w_c, :] = y.T.astype(o_ref.dtype)
            else:
                o_ref[:, c0:c0 + w_c] = y.astype(o_ref.dtype)


def in_proj(lay, h, mod, g, w, segs, out_dtypes, *, rope=None, rope_cols=0, shift_row=0, scale_row=1,
            cw=512, out_rows=None, name="in_proj"):
    tm, D = lay.tm, h.shape[1]
    N = w.shape[1]
    out_rows = out_rows or [None] * len(segs)
    out_rows = [r or (lay.T, lambda i: i, False) for r in out_rows]
    in_specs = [pl.BlockSpec((tm, D), lambda i: (i, 0)),
                pl.BlockSpec((None, N_MOD, D), lambda i: (lay.mod_idx(i), 0, 0)),
                pl.BlockSpec((1, D), lambda i: (0, 0)),
                _resident((D, N))]
    args = [h, mod, g.reshape(1, D), w]
    if rope_cols:
        cos, sin = rope
        in_specs += [pl.BlockSpec((tm, LANES), lambda i: (lay.pos_idx(i), 0))] * 2
        args += [cos, sin]
    out_specs, out_shape = [], []
    for (_, width), dt, (n_rows, blk, tr) in zip(segs, out_dtypes, out_rows):
        if tr:
            out_specs.append(pl.BlockSpec((width, tm), lambda i, blk=blk: (0, blk(i))))
            out_shape.append(jax.ShapeDtypeStruct((width, n_rows), dt))
        else:
            out_specs.append(pl.BlockSpec((tm, width), lambda i, blk=blk: (blk(i), 0)))
            out_shape.append(jax.ShapeDtypeStruct((n_rows, width), dt))
    kern = functools.partial(_in_proj_kernel, segs=tuple(segs), transposed=tuple(r[2] for r in out_rows),
                             rope_cols=rope_cols, shift_row=shift_row, scale_row=scale_row, cw=cw)
    return pl.pallas_call(
        kern, grid=(lay.nblk,), in_specs=in_specs, out_specs=out_specs, out_shape=out_shape,
        compiler_params=_cparams(("parallel",)), name=name,
    )(*args)


def _out_proj_kernel(*refs, ssd, gate_row):
    if ssd:
        yf_ref, yb_ref, xbc_ref, z_ref, dskip_ref, gn_ref, w_ref, h_ref, mod_ref, g_ref, o_ref = refs
        inner = yf_ref.shape[1]
        gw = inner // SSD_GROUPS
        y = (yf_ref[...].astype(F32) + yb_ref[...].astype(F32)
             + xbc_ref[:, :inner].astype(F32) * dskip_ref[...])
        y = y * _silu(z_ref[...].astype(F32))
        y = jnp.concatenate([_rms(y[:, k * gw:(k + 1) * gw]) for k in range(SSD_GROUPS)], axis=1)
        a = (y * gn_ref[...]).astype(BF16)
    else:
        a_ref, w_ref, h_ref, mod_ref, g_ref, o_ref = refs
        a = a_ref[...]
    y = jnp.dot(a, w_ref[...], preferred_element_type=F32)
    o_ref[...] = h_ref[...] + mod_ref[gate_row:gate_row + 1, :] * (_rms(y) * g_ref[...])


def out_proj(lay, acts, w, h, mod, g, *, ssd_extra=None, gate_row=2, nblk=None, name="out_proj"):
    tm, D = lay.tm, h.shape[1]
    K = w.shape[0]
    nblk = lay.nblk if nblk is None else nblk
    row = lambda width: pl.BlockSpec((tm, width), lambda i: (i, 0))
    in_specs = [row(a.shape[1]) for a in acts]
    args = list(acts)
    if ssd_extra is not None:
        dskip, gn = ssd_extra
        in_specs += [pl.BlockSpec((1, K), lambda i: (0, 0))] * 2
        args += [dskip, gn]
    in_specs += [_resident((K, D)), row(D),
                 pl.BlockSpec((None, N_MOD, D), lambda i: (lay.mod_idx(i), 0, 0)),
                 pl.BlockSpec((1, D), lambda i: (0, 0))]
    args += [w, h, mod, g.reshape(1, D)]
    kern = functools.partial(_out_proj_kernel, ssd=ssd_extra is not None, gate_row=gate_row)
    return pl.pallas_call(
        kern, grid=(nblk,), in_specs=in_specs, out_specs=row(D),
        out_shape=jax.ShapeDtypeStruct((nblk * tm, D), F32),
        compiler_params=_cparams(("parallel",)), name=name,
    )(*args)


def _ffn_kernel(*refs, hidden, cw, with_mixer):
    if with_mixer:
        a_ref, wo_ref, h_ref, mod_ref, g_ref, w_in_ref, w_out_ref, o_ref, act_ref = refs
        y = jnp.dot(a_ref[...], wo_ref[...], preferred_element_type=F32)
        h = h_ref[...] + mod_ref[2:3, :] * (_rms(y) * g_ref[1:2, :])
    else:
        h_ref, mod_ref, g_ref, w_in_ref, w_out_ref, o_ref, act_ref = refs
        h = h_ref[...]
    u = _rms(h) * g_ref[2:3, :]
    ub = (u * (1.0 + mod_ref[4:5, :]) + mod_ref[3:4, :]).astype(BF16)
    for c0 in range(0, hidden, cw):
        gate = jnp.dot(ub, w_in_ref[:, c0:c0 + cw], preferred_element_type=F32)
        up = jnp.dot(ub, w_in_ref[:, hidden + c0:hidden + c0 + cw], preferred_element_type=F32)
        act_ref[:, c0:c0 + cw] = (_silu(gate) * up).astype(BF16)
    y = jnp.dot(act_ref[...], w_out_ref[...], preferred_element_type=F32)
    o_ref[...] = h + mod_ref[5:6, :] * (_rms(y) * g_ref[3:4, :])


def ffn(lay, h, mod, g4, w_in, w_out, *, mixer=None, nblk=None, name="ffn"):
    tm, D = lay.tm, h.shape[1]
    hidden = w_out.shape[0]
    nblk = lay.nblk if nblk is None else nblk
    cw = 256
    assert hidden % cw == 0
    row = lambda width: pl.BlockSpec((tm, width), lambda i: (i, 0))
    in_specs, args = [], []
    if mixer is not None:
        a, w_o = mixer
        in_specs += [row(a.shape[1]), _resident(w_o.shape)]
        args += [a, w_o]
    in_specs += [row(D), pl.BlockSpec((None, N_MOD, D), lambda i: (lay.mod_idx(i), 0, 0)),
                 pl.BlockSpec((4, D), lambda i: (0, 0)),
                 _resident((D, 2 * hidden)), _resident((hidden, D))]
    args += [h, mod, g4, w_in, w_out]
    kern = functools.partial(_ffn_kernel, hidden=hidden, cw=cw, with_mixer=mixer is not None)
    return pl.pallas_call(
        kern, grid=(nblk,), in_specs=in_specs, out_specs=row(D),
        out_shape=jax.ShapeDtypeStruct((nblk * tm, D), F32),
        scratch_shapes=[pltpu.VMEM((tm, hidden), BF16)],
        compiler_params=_cparams(("parallel",)), name=name,
    )(*args)


def _da_attn_kernel(lam_ref, subln_ref, q_ref, k_ref, vt_ref, o_ref, *scratch, lam_init, tk, n_chunks, n_tiles):
    tq = q_ref.shape[0] // n_tiles
    ns = 2 * n_tiles
    accs = scratch[:ns]
    s_bufs = (scratch[ns:2 * ns], scratch[2 * ns:3 * ns])
    p_bufs = (scratch[3 * ns:4 * ns], scratch[4 * ns:5 * ns])
    qts = []
    for t in range(n_tiles):
        qt = q_ref[t * tq:(t + 1) * tq, :].astype(F32).T
        row = lax.broadcasted_iota(jnp.int32, qt.shape, 0)
        qts += [jnp.where(row < HEAD_DIM, qt, 0.0).astype(BF16), jnp.where(row >= HEAD_DIM, qt, 0.0).astype(BF16)]

    def scores(c, slot):
        kc = k_ref[pl.ds(pl.multiple_of(c * tk, tk), tk), :]
        for st in range(ns):
            s_bufs[slot][st][...] = jnp.dot(kc, qts[st], preferred_element_type=F32)

    def accumulate(c, slot, alphas):
        vt = jnp.concatenate([vt_ref[:, pl.ds(pl.multiple_of(c * tk, tk), tk)],
                              jnp.ones((DA_SUM_ROWS, tk), BF16)], axis=0)
        for st in range(ns):
            accs[st][...] = alphas[st] * accs[st][...] + jnp.dot(vt, p_bufs[slot][st][...],
                                                                 preferred_element_type=F32)

    def softmax(slot, maxes):
        new, alphas = [], []
        for st in range(ns):
            s = s_bufs[slot][st][...]
            m_new = jnp.maximum(maxes[st], jnp.max(s, axis=0, keepdims=True))
            p_bufs[slot][st][...] = jnp.exp2(s - m_new).astype(BF16)
            new.append(m_new)
            alphas.append(jnp.exp2(maxes[st] - m_new))
        return tuple(new), tuple(alphas)

    for st in range(ns):
        accs[st][...] = jnp.zeros_like(accs[st])
        p_bufs[1][st][...] = jnp.zeros_like(p_bufs[1][st])
    scores(0, 0)
    stats = (jnp.full((1, tq), NEG_INF, F32),) * ns
    ones = (jnp.ones((1, tq), F32),) * ns
    assert n_chunks % 2 == 1 and k_ref.shape[0] == n_chunks * tk

    n_pairs = (n_chunks - 1) // 2
    unroll = math.gcd(n_pairs, max(DA_UNROLL // n_tiles, 1)) if n_pairs else 1

    def pairs(t, carry):
        stats, alphas1 = carry
        for u in range(unroll):
            c = 2 * (t * unroll + u)
            scores(c + 1, 1)
            accumulate(jnp.maximum(c - 1, 0), 1, alphas1)
            stats, alphas0 = softmax(0, stats)
            scores(c + 2, 0)
            accumulate(c, 0, alphas0)
            stats, alphas1 = softmax(1, stats)
        return stats, alphas1

    stats, alphas1 = lax.fori_loop(0, n_pairs // unroll, pairs, (stats, ones))
    last = n_chunks - 1
    accumulate(jnp.maximum(last - 1, 0), 1, alphas1)
    stats, alphas0 = softmax(0, stats)
    accumulate(last, 0, alphas0)

    lp = lam_ref[...]
    lam = (jnp.exp(jnp.sum(lp[0:1] * lp[1:2], axis=-1, keepdims=True))
           - jnp.exp(jnp.sum(lp[2:3] * lp[3:4], axis=-1, keepdims=True)) + lam_init)
    for t in range(n_tiles):
        a0, a1 = accs[2 * t][...], accs[2 * t + 1][...]
        ot = a0[:LANES] / a0[LANES:LANES + 1] - lam * (a1[:LANES] / a1[LANES:LANES + 1])
        o = _rms(ot.T) * subln_ref[...] * (1.0 - lam_init)
        o_ref[t * tq:(t + 1) * tq, :] = o.astype(o_ref.dtype)


def da_attention(B, S, Lc, q_src, k_all, vt_all, lam_p, subln, lam_init, want_ctx):
    T = q_src.shape[0]
    width = DA_HEADS * 2 * HEAD_DIM
    n_keys = Lc + S
    small = (jnp.asarray(lam_p), subln.reshape(1, 2 * HEAD_DIM))
    small_specs = [pl.BlockSpec((4, HEAD_DIM), lambda b, h, i: (0, 0)),
                   pl.BlockSpec((1, 2 * HEAD_DIM), lambda b, h, i: (0, 0))]

    def call(tq, n_tiles, tk, n_rows_k, k_blocks_per_batch, k_block0, nq, q_block0, prev):
        n_chunks = n_rows_k // tk
        ns = 2 * n_tiles
        kern = functools.partial(_da_attn_kernel, lam_init=lam_init, tk=tk, n_chunks=n_chunks, n_tiles=n_tiles)
        q_idx = lambda b, h, i: (q_block0 + b * nq + i, h)
        in_specs = small_specs + [pl.BlockSpec((n_tiles * tq, LANES), q_idx),
                                  pl.BlockSpec((n_rows_k, LANES),
                                               lambda b, h, i: (b * k_blocks_per_batch + k_block0, h)),
                                  pl.BlockSpec((LANES, n_rows_k),
                                               lambda b, h, i: (h, b * k_blocks_per_batch + k_block0))]
        args = small + (q_src, k_all, vt_all)
        aliases = {}
        body = kern
        if prev is not None:
            in_specs.append(pl.BlockSpec(memory_space=pl.ANY))
            args += (prev,)
            aliases = {len(args) - 1: 0}
            body = lambda *refs: kern(*refs[:5], *refs[6:])
        return pl.pallas_call(
            body, grid=(B, DA_HEADS, nq), in_specs=in_specs,
            out_specs=pl.BlockSpec((n_tiles * tq, LANES), q_idx),
            out_shape=jax.ShapeDtypeStruct((T, width), BF16),
            scratch_shapes=([pltpu.VMEM((LANES + DA_SUM_ROWS, tq), F32)] * ns + [pltpu.VMEM((tk, tq), F32)] * (2 * ns)
                            + [pltpu.VMEM((tk, tq), BF16)] * (2 * ns)),
            input_output_aliases=aliases,
            compiler_params=_cparams(("parallel", "parallel", "arbitrary")),
            name="da_attention" if prev is None else "da_attention_ctx",
        )(*args)

    tq = min(DA_TQ, S)
    n_tiles = DA_TILES if S % (DA_TILES * tq) == 0 else 1
    tk = next(t for t in DA_TK if n_keys % t == 0 and (n_keys // t) % 2 == 1)
    o = call(tq, n_tiles, tk, n_keys, 1, 0, S // (n_tiles * tq), 0, None)
    if want_ctx:
        assert S % Lc == 0
        o = call(Lc, 1, Lc, Lc, n_keys // Lc, S // Lc, 1, B * S // Lc, o)
    return o


def _sw_attn_kernel(sink_ref, q_ref, kc_ref, vc_ref, kp_ref, kx_ref, kn_ref, vp_ref, vx_ref, vn_ref,
                    o_ref, *, n_lat_q, nb):
    tq = q_ref.shape[0]
    Lc = kc_ref.shape[0]
    i = pl.program_id(1)
    is_lat = i < n_lat_q
    n = i % nb
    krel = lax.broadcasted_iota(jnp.int32, (3 * SW_BLOCK, tq), 0) - SW_BLOCK
    qrow = lax.broadcasted_iota(jnp.int32, (3 * SW_BLOCK, tq), 1)
    kabs = krel + n * SW_BLOCK
    valid = (jnp.abs(qrow - krel) <= SW_WINDOW) & (kabs >= 0) & (kabs < nb * SW_BLOCK) & is_lat
    valid = jnp.concatenate([valid] * SW_GROUP, axis=1)
    row = lax.broadcasted_iota(jnp.int32, (LANES, SW_GROUP * tq), 0)
    lo = row < HEAD_DIM
    for j in range(SW_KV_HEADS // 2):
        ksl = slice(j * LANES, (j + 1) * LANES)
        keys = jnp.concatenate([kc_ref[:, ksl], kp_ref[:, ksl], kx_ref[:, ksl], kn_ref[:, ksl]], axis=0)
        vals = jnp.concatenate([vc_ref[:, ksl], vp_ref[:, ksl], vx_ref[:, ksl], vn_ref[:, ksl]], axis=0)
        vals_t = vals.astype(F32).T.astype(BF16)
        qt = jnp.concatenate(
            [q_ref[:, (j * SW_GROUP + g) * LANES:(j * SW_GROUP + g + 1) * LANES].astype(F32).T
             for g in range(SW_GROUP)], axis=1)
        halves = []
        for e in range(2):
            sink = sink_ref[2 * j + e:2 * j + e + 1, :]
            qz = jnp.where(lo if e == 0 else ~lo, qt, 0.0).astype(BF16)
            s = jnp.dot(keys, qz, preferred_element_type=F32)
            s_ctx = s[:Lc]
            s_loc = jnp.where(valid, s[Lc:], NEG_INF)
            m = jnp.maximum(jnp.maximum(jnp.max(s_ctx, axis=0, keepdims=True),
                                        jnp.max(s_loc, axis=0, keepdims=True)), sink)
            p_ctx = jnp.exp(s_ctx - m)
            p_loc = jnp.exp(s_loc - m)
            denom = (jnp.sum(p_ctx, axis=0, keepdims=True) + jnp.sum(p_loc, axis=0, keepdims=True)
                     + jnp.exp(sink - m))
            p = jnp.concatenate([p_ctx, p_loc], axis=0).astype(BF16)
            halves.append(jnp.dot(vals_t, p, preferred_element_type=F32) / denom)
        ot = jnp.where(lo, halves[0], halves[1])
        for g in range(SW_GROUP):
            slab = j * SW_GROUP + g
            o_ref[:, slab * LANES:(slab + 1) * LANES] = ot[:, g * tq:(g + 1) * tq].T.astype(o_ref.dtype)


def sw_attention(B, S, Lc, qkv, sink, want_ctx):
    T = qkv.shape[0]
    qw = SW_HEADS * HEAD_DIM
    kvw = SW_KV_HEADS * HEAD_DIM
    tq = SW_BLOCK
    nb = S // SW_BLOCK
    nqc = Lc // tq
    nq = nb + (nqc if want_ctx else 0)
    kcol, vcol = qw // kvw, qw // kvw + 1

    def q_idx(b, i):
        return (jnp.where(i < nb, b * nb + i, B * nb + b * nqc + (i - nb)), 0)

    def band(col, off):
        def idx(b, i):
            n = jnp.clip(jnp.where(i < nb, i, 0) + off, 0, nb - 1)
            return (b * nb + n, col)
        return pl.BlockSpec((SW_BLOCK, kvw), idx)

    sink_lanes = jnp.repeat(sink.astype(F32).reshape(SW_KV_HEADS, SW_GROUP), tq, axis=1)
    kern = functools.partial(_sw_attn_kernel, n_lat_q=nb, nb=nb)
    return pl.pallas_call(
        kern, grid=(B, nq),
        in_specs=[pl.BlockSpec((SW_KV_HEADS, SW_GROUP * tq), lambda b, i: (0, 0)),
                  pl.BlockSpec((tq, qw), q_idx),
                  pl.BlockSpec((Lc, kvw), lambda b, i: (B * S // Lc + b, kcol)),
                  pl.BlockSpec((Lc, kvw), lambda b, i: (B * S // Lc + b, vcol)),
                  band(kcol, -1), band(kcol, 0), band(kcol, 1),
                  band(vcol, -1), band(vcol, 0), band(vcol, 1)],
        out_specs=pl.BlockSpec((tq, qw), q_idx),
        out_shape=jax.ShapeDtypeStruct((T, qw), BF16),
        compiler_params=_cparams(("parallel", "arbitrary")), name="sw_attention",
    )(sink_lanes, qkv, qkv, qkv, qkv, qkv, qkv, qkv, qkv, qkv)


def _sw_head_perm():
    cols = []
    for j in range(SW_KV_HEADS // 2):
        for g in range(SW_GROUP):
            for e in range(2):
                head = (2 * j + e) * SW_GROUP + g
                cols += list(range(head * HEAD_DIM, (head + 1) * HEAD_DIM))
    return np.asarray(cols, dtype=np.int32)


def _ssd_conv_kernel(prev_ref, cur_ref, next_ref, w_ref, b_ref, o_ref, ext_ref, *, seg_lat, seg_ctx, n_lat_blk):
    tc = cur_ref.shape[0]
    i = pl.program_id(0)
    row0 = i * tc
    is_lat = i < n_lat_blk
    seg = jnp.where(is_lat, seg_lat, seg_ctx)
    rel = jnp.where(is_lat, row0, row0 - n_lat_blk * tc) % seg
    keep_prev = (rel != 0).astype(F32)
    keep_next = (rel + tc != seg).astype(F32)
    ext_ref[0:CONV_HALO, :] = prev_ref[...].astype(F32) * keep_prev
    ext_ref[CONV_HALO:CONV_HALO + tc, :] = cur_ref[...].astype(F32)
    ext_ref[CONV_HALO + tc:, :] = next_ref[...].astype(F32) * keep_next
    acc = jnp.zeros(o_ref.shape, F32) + b_ref[...]
    ext = ext_ref[...]
    n_ext = ext.shape[0]
    for k in range(SSD_CONV):
        shifted = ext if k == SSD_CONV // 2 else pltpu.roll(ext, (SSD_CONV // 2 - k) % n_ext, 0)
        acc = acc + shifted[CONV_HALO:CONV_HALO + tc, :] * w_ref[k:k + 1, :]
    o_ref[...] = _silu(acc).astype(o_ref.dtype)


def ssd_conv(B, S, Lc, xbc, conv_w, conv_b):
    T, C = xbc.shape
    tc = min(256, Lc)
    cwid = next(w for w in (1536, 1024, 512) if C % w == 0)
    nblk = T // tc
    hb = tc // CONV_HALO
    nh = T // CONV_HALO
    kern = functools.partial(_ssd_conv_kernel, seg_lat=S, seg_ctx=Lc, n_lat_blk=B * S // tc)
    return pl.pallas_call(
        kern, grid=(nblk, C // cwid),
        in_specs=[pl.BlockSpec((CONV_HALO, cwid), lambda i, j: (jnp.maximum(i * hb - 1, 0), j)),
                  pl.BlockSpec((tc, cwid), lambda i, j: (i, j)),
                  pl.BlockSpec((CONV_HALO, cwid), lambda i, j: (jnp.minimum((i + 1) * hb, nh - 1), j)),
                  pl.BlockSpec((SSD_CONV, cwid), lambda i, j: (0, j)),
                  pl.BlockSpec((1, cwid), lambda i, j: (0, j))],
        out_specs=pl.BlockSpec((tc, cwid), lambda i, j: (i, j)),
        out_shape=jax.ShapeDtypeStruct((T, C), BF16),
        scratch_shapes=[pltpu.VMEM((tc + 2 * CONV_HALO, cwid), F32)],
        compiler_params=_cparams(("parallel", "parallel")), name="ssd_conv",
    )(xbc, xbc, xbc, conv_w, conv_b.reshape(1, C))


def _split3(t):
    hi = t.astype(BF16)
    r1 = t - hi.astype(F32)
    mid = r1.astype(BF16)
    lo = (r1 - mid.astype(F32)).astype(BF16)
    return hi, mid, lo


def _dot01(t, ones_mat):
    out = None
    for part in _split3(t):
        term = jnp.dot(part, ones_mat, preferred_element_type=F32)
        out = term if out is None else out + term
    return out


def _dot_left01(ones_mat, t):
    out = None
    for part in _split3(t):
        term = jnp.dot(ones_mat, part, preferred_element_type=F32)
        out = term if out is None else out + term
    return out


def _ssd_scan_kernel(xf_ref, xb_ref, dtf_ref, dtb_ref, aneg_ref, bias_ref, ef_ref, eb_ref,
                     yf_ref, yb_ref, sf_ref, sb_ref, *, inner, n_heads):
    @pl.when(pl.program_id(1) == 0)
    def _():
        sf_ref[...] = jnp.zeros_like(sf_ref)
        sb_ref[...] = jnp.zeros_like(sb_ref)

    for reverse, x_ref, dt_ref, e_ref, y_ref, s_ref in (
            (False, xf_ref, dtf_ref, ef_ref, yf_ref, sf_ref),
            (True, xb_ref, dtb_ref, eb_ref, yb_ref, sb_ref)):
        _ssd_chunk(x_ref, dt_ref, aneg_ref, bias_ref, e_ref, y_ref, s_ref,
                   reverse=reverse, inner=inner, n_heads=n_heads)


def _ssd_chunk(x_ref, dt_ref, aneg_ref, bias_ref, e_ref, y_ref, s_ref, *, reverse, inner, n_heads):
    L = x_ref.shape[0]
    gn = SSD_GROUPS * SSD_STATE
    hpg = n_heads // SSD_GROUPS
    gw = hpg * SSD_HEAD_DIM
    head0 = n_heads if reverse else 0

    z = dt_ref[...] + bias_ref[...]
    dt = jnp.maximum(z, 0.0) + jnp.log(1.0 + jnp.exp(-jnp.abs(z)))
    a = dt * (-jnp.exp(aneg_ref[...]))
    r = lax.broadcasted_iota(jnp.int32, (L, L), 0)
    c = lax.broadcasted_iota(jnp.int32, (L, L), 1)
    causal = (c >= r) if reverse else (c <= r)
    tri = jnp.where(causal, 1.0, 0.0).astype(BF16)
    acs = _dot_left01(tri, a)
    a_tot = jnp.sum(a, axis=0, keepdims=True)
    dte = jnp.exp(a_tot - acs)
    eacs = jnp.exp(acs)
    acs_t = acs.T

    dt_t = dt.T

    pk = 2 * SUBLANES
    eacs_hi, eacs_mid, _ = _split3(eacs)
    cdec_parts = [jnp.broadcast_to(part, (pk, LANES)) for part in _split3(jnp.exp(a_tot))]
    stack = jnp.concatenate([(dt * dte).astype(BF16), eacs_hi, eacs_mid] + cdec_parts, axis=0)
    ex = jnp.dot(stack, e_ref[...], preferred_element_type=F32)
    dtdte_e = ex[0:L]
    eacs_e = ex[L:2 * L] + ex[2 * L:3 * L]
    cdec_e = ex[3 * L:3 * L + 1] + ex[3 * L + pk:3 * L + pk + 1] + ex[3 * L + 2 * pk:3 * L + 2 * pk + 1]

    xw = (x_ref[:, :inner].astype(F32) * dtdte_e).astype(BF16)
    lane = lax.broadcasted_iota(jnp.int32, (L, LANES), 1)
    lo = lane < SSD_HEAD_DIM
    nt = (((1,), (1,)), ((), ()))
    tn = (((0,), (0,)), ((), ()))
    for g in range(SSD_GROUPS):
        Bg = x_ref[:, inner + g * SSD_STATE:inner + (g + 1) * SSD_STATE]
        Cg = x_ref[:, inner + gn + g * SSD_STATE:inner + gn + (g + 1) * SSD_STATE]
        gs = slice(g * gw, (g + 1) * gw)
        cb = lax.dot_general(Cg, Bg, nt, preferred_element_type=F32)
        s_prev = s_ref[:, gs]
        y_off = jnp.dot(Cg, s_prev.astype(BF16), preferred_element_type=F32) * eacs_e[:, gs]
        for k in range(hpg // 2):
            sl = slice(g * gw + k * LANES, g * gw + (k + 1) * LANES)
            halves = []
            for eh in range(2):
                col = head0 + g * hpg + 2 * k + eh
                seg = acs[:, col:col + 1] - acs_t[col:col + 1, :]
                w = cb * jnp.exp(jnp.where(causal, seg, NEG_INF)) * dt_t[col:col + 1, :]
                halves.append(jnp.dot(w.astype(BF16), x_ref[:, sl], preferred_element_type=F32))
            y = jnp.where(lo, halves[0], halves[1]) + y_off[:, k * LANES:(k + 1) * LANES]
            y_ref[:, sl] = y.astype(y_ref.dtype)
        s_ref[:, gs] = s_prev * cdec_e[:, gs] + lax.dot_general(Bg, xw[:, gs], tn, preferred_element_type=F32)


def ssd_scan(B, S, Lc, xbc, dt_raw, a_log_pad, dt_bias_pad, inner, n_heads):
    T, C = xbc.shape
    L = SSD_CHUNK
    nl, nc = S // L, Lc // L
    nsteps = nl + nc

    def fwd_idx(b, i):
        return (jnp.where(i < nc, B * nl + b * nc + i, b * nl + (i - nc)), 0)

    def bwd_idx(b, i):
        return (jnp.where(i < nc, B * nl + b * nc + (nc - 1 - i), b * nl + (nl - 1 - (i - nc))), 0)

    hd = np.arange(inner) // SSD_HEAD_DIM
    ef = (np.arange(LANES)[:, None] == hd[None, :]).astype(np.float32)
    eb = (np.arange(LANES)[:, None] == (hd[None, :] + n_heads)).astype(np.float32)
    kern = functools.partial(_ssd_scan_kernel, inner=inner, n_heads=n_heads)
    const = lambda shape: pl.BlockSpec(shape, lambda b, i: (0, 0))
    return pl.pallas_call(
        kern, grid=(B, nsteps),
        in_specs=[pl.BlockSpec((L, C), fwd_idx), pl.BlockSpec((L, C), bwd_idx),
                  pl.BlockSpec((L, LANES), fwd_idx), pl.BlockSpec((L, LANES), bwd_idx),
                  const((1, LANES)), const((1, LANES)), const((LANES, inner)), const((LANES, inner))],
        out_specs=[pl.BlockSpec((L, inner), fwd_idx), pl.BlockSpec((L, inner), bwd_idx)],
        out_shape=[jax.ShapeDtypeStruct((T, inner), BF16)] * 2,
        scratch_shapes=[pltpu.VMEM((SSD_STATE, inner), F32)] * 2,
        compiler_params=_cparams(("parallel", "arbitrary")), name="ssd_scan",
    )(xbc, xbc, dt_raw, dt_raw, a_log_pad, dt_bias_pad, jnp.asarray(ef, BF16), jnp.asarray(eb, BF16))


def _rope_tables(S, tm):
    n_rows = S // GRID_W
    rows = jnp.repeat(jnp.arange(n_rows), GRID_W)
    cols = jnp.tile(jnp.arange(GRID_W), n_rows)
    n_freq = HEAD_DIM // 4
    freqs = ROPE_THETA ** (-jnp.arange(n_freq, dtype=F32) / n_freq)
    ang = jnp.concatenate([rows[:, None].astype(F32) * freqs, cols[:, None].astype(F32) * freqs], axis=-1)
    cos, sin = jnp.cos(ang), jnp.sin(ang)
    cos_t = jnp.concatenate([cos, cos, cos, cos], axis=-1)
    sin_t = jnp.concatenate([-sin, sin, -sin, sin], axis=-1)
    cos_t = jnp.concatenate([cos_t, jnp.ones((tm, LANES), F32)], axis=0)
    sin_t = jnp.concatenate([sin_t, jnp.zeros((tm, LANES), F32)], axis=0)
    return cos_t, sin_t


def kernel(x, c, ctx, c_ctx, ada_w, ada_b, norm_g, ffn_w_in, ffn_w_out, da_w_qkv, da_w_o, da_lambda, da_subln,
           sw_w_qkv, sw_w_o, sw_sink, ssd_w_in, ssd_conv_w, ssd_conv_b, ssd_a_log, ssd_dt_bias, ssd_d_skip,
           ssd_norm, ssd_w_out):
    B, S, D = x.shape
    Lc = ctx.shape[1]
    depth = ada_w.shape[0]
    tm = min(512, math.gcd(S, B * Lc))
    lay = Layout(B, S, Lc, tm)

    n_cond = -(-(B + 1) // SUBLANES) * SUBLANES
    cond = jnp.concatenate([c, c_ctx[None, :], jnp.zeros((n_cond - B - 1, D), F32)], axis=0)
    mods = ada_modulation(cond, ada_w, ada_b)
    rope = _rope_tables(S, tm)
    lay_kv = Layout(B, S, Lc, min(256, math.gcd(S, Lc)))
    rope_kv = _rope_tables(S, lay_kv.tm)

    h = jnp.concatenate([x.reshape(B * S, D), ctx.reshape(B * Lc, D)], axis=0)
    da_width = DA_HEADS * 2 * HEAD_DIM
    sw_qw = SW_HEADS * HEAD_DIM
    sw_kvw = SW_KV_HEADS * HEAD_DIM
    perm = _sw_head_perm()

    for i in range(depth):
        want_ctx = i < depth - 1
        j = i // N_MIXERS
        kind = i % N_MIXERS
        mod = mods[i]
        g = norm_g[i]
        nblk = None if want_ctx else lay.nlat
        if kind == 0:
            lam_init = 0.8 - 0.6 * math.exp(-0.3 * i)
            w = da_w_qkv[j]
            q_scale = HEAD_DIM ** -0.5 * math.log2(math.e)
            w = jnp.concatenate([w[:, :da_width] * q_scale, w[:, da_width:]], axis=1).astype(BF16)
            n_keys = S + Lc
            kv_rows = (B * n_keys, lay_kv.kv_block, False)
            q, k_all, vt_all = in_proj(
                lay_kv, h, mod, g[0], w, [(0, da_width), (da_width, da_width), (2 * da_width, da_width)],
                [BF16] * 3, rope=rope_kv, rope_cols=2 * da_width,
                out_rows=[None, kv_rows, (B * n_keys, lay_kv.kv_block, True)], name="da_in_proj")
            o = da_attention(B, S, Lc, q, k_all, vt_all, da_lambda[j], da_subln[j], lam_init, want_ctx)
            mixer = (o, da_w_o[j].astype(BF16))
        elif kind == 1:
            w = sw_w_qkv[j]
            w = jnp.concatenate([w[:, :sw_qw][:, perm] * HEAD_DIM ** -0.5, w[:, sw_qw:]], axis=1).astype(BF16)
            (qkv,) = in_proj(lay, h, mod, g[0], w, [(0, sw_qw + 2 * sw_kvw)], [BF16], rope=rope,
                             rope_cols=sw_qw + sw_kvw, cw=256, name="sw_in_proj")
            o = sw_attention(B, S, Lc, qkv, sw_sink[j], want_ctx)
            mixer = (o, sw_w_o[j][perm, :].astype(BF16))
        else:
            n_heads = ssd_a_log.shape[2]
            inner = n_heads * SSD_HEAD_DIM
            conv_ch = inner + 2 * SSD_GROUPS * SSD_STATE
            w = ssd_w_in[j]
            w = jnp.pad(w, ((0, 0), (0, LANES - 2 * n_heads))).astype(BF16)
            z, xbc, dt_raw = in_proj(lay, h, mod, g[0], w,
                                     [(0, inner), (inner, conv_ch), (inner + conv_ch, LANES)],
                                     [BF16, BF16, F32], name="ssd_in_proj")
            xbc = ssd_conv(B, S, Lc, xbc, ssd_conv_w[j], ssd_conv_b[j])
            pad = lambda t: jnp.pad(t.reshape(1, 2 * n_heads), ((0, 0), (0, LANES - 2 * n_heads)))
            yf, yb = ssd_scan(B, S, Lc, xbc, dt_raw, pad(ssd_a_log[j]), pad(ssd_dt_bias[j]), inner, n_heads)
            dskip = jnp.repeat(ssd_d_skip[j], SSD_HEAD_DIM).reshape(1, inner)
            h = out_proj(lay, [yf, yb, xbc, z], ssd_w_out[j].astype(BF16), h, mod, g[1],
                         ssd_extra=(dskip, ssd_norm[j].reshape(1, inner)), nblk=nblk, name="ssd_out_proj")
            mixer = None
        h = ffn(lay, h, mod, g, ffn_w_in[i].astype(BF16), ffn_w_out[i].astype(BF16), mixer=mixer, nblk=nblk)
    return h[:B * S].reshape(B, S, D)
```
